```python
import math, functools
import jax, jax.numpy as jnp
from jax import lax
import numpy as np

D_MODEL = 2048
BATCH = 4
SEQ = 2048
DEPTH = 4

N_MIXERS = 3
N_RGLRU_LAYERS = (DEPTH + 2) // 3
N_GDN_LAYERS = (DEPTH + 1) // 3
N_GLA_LAYERS = DEPTH // 3

DEEPNORM_ALPHA = float((2 * DEPTH) ** 0.25)
DEEPNORM_BETA = float((8 * DEPTH) ** -0.25)
LN_EPS = 1e-5
RMS_EPS = 1e-6
D_FF = 4 * D_MODEL
N_MOD = 6
CONV_WIDTH = 4

RNN_WIDTH = (5 * D_MODEL) // 4
RNN_BLOCKS = 16
RNN_BLOCK_DIM = RNN_WIDTH // RNN_BLOCKS
LRU_C = 8.0

GDN_HEAD_DIM = 128
GDN_QK_HEADS = D_MODEL // GDN_HEAD_DIM
GDN_V_HEADS = 2 * GDN_QK_HEADS
GDN_KEY_DIM = GDN_QK_HEADS * GDN_HEAD_DIM
GDN_VALUE_DIM = GDN_V_HEADS * GDN_HEAD_DIM
GDN_CONV_DIM = 2 * GDN_KEY_DIM + GDN_VALUE_DIM
GDN_IN_DIM = GDN_CONV_DIM + GDN_VALUE_DIM + 2 * GDN_V_HEADS
GDN_CHUNK = 64

GLA_HEADS = 4
GLA_KEY_DIM = D_MODEL // 2
GLA_VALUE_DIM = D_MODEL
GLA_HEAD_K = GLA_KEY_DIM // GLA_HEADS
GLA_HEAD_V = GLA_VALUE_DIM // GLA_HEADS
GLA_GATE_RANK = 16
GLA_GATE_NORMALIZER = 16.0
GLA_IN_DIM = 2 * GLA_KEY_DIM + 2 * GLA_VALUE_DIM + GLA_GATE_RANK
GLA_CHUNK = 64

kernel_name = 'hybrid_rglru_gdn_gla_deepnorm_adaln'


def _layer_norm(x, g, b):
    xf = x.astype(jnp.float32)
    mu = jnp.mean(xf, axis=-1, keepdims=True)
    var = jnp.mean(jnp.square(xf - mu), axis=-1, keepdims=True)
    return ((xf - mu) * lax.rsqrt(var + LN_EPS) * g.astype(jnp.float32) + b.astype(jnp.float32)).astype(x.dtype)


def _rms_norm_gated(o, w, z):
    of = o.astype(jnp.float32)
    n = of * lax.rsqrt(jnp.mean(of * of, axis=-1, keepdims=True) + RMS_EPS)
    return n * w.astype(jnp.float32) * jax.nn.silu(z.astype(jnp.float32))


def _l2norm(t):
    return t * lax.rsqrt(jnp.sum(t * t, axis=-1, keepdims=True) + RMS_EPS)


def _causal_depthwise_conv(x, w):
    k = w.shape[0]
    return lax.conv_general_dilated(
        x, w[:, None, :].astype(x.dtype), window_strides=(1,), padding=((k - 1, 0),),
        dimension_numbers=('NWC', 'WIO', 'NWC'), feature_group_count=x.shape[-1])


def _linear_scan(a, b):
    def combine(left, right):
        a_l, b_l = left
        a_r, b_r = right
        return a_l * a_r, a_r * b_l + b_r
    return lax.associative_scan(combine, (a, b), axis=1)[1]


def rglru_mixer(h, w_in, conv_w, conv_b, w_rgate, b_rgate, w_igate, b_igate, lam, w_out):
    bsz, seq, _ = h.shape
    gate_branch, rec_branch = jnp.split(h @ w_in, 2, axis=-1)
    xr = _causal_depthwise_conv(rec_branch, conv_w) + conv_b
    xb = xr.reshape(bsz, seq, RNN_BLOCKS, RNN_BLOCK_DIM)
    r = jax.nn.sigmoid((jnp.einsum('bsnd,nde->bsne', xb, w_rgate) + b_rgate).astype(jnp.float32))
    i = jax.nn.sigmoid((jnp.einsum('bsnd,nde->bsne', xb, w_igate) + b_igate).astype(jnp.float32))
    log_a = -LRU_C * r * jax.nn.softplus(-lam.astype(jnp.float32)).reshape(RNN_BLOCKS, RNN_BLOCK_DIM)
    a = jnp.exp(log_a)
    b_t = jnp.sqrt(-jnp.expm1(2.0 * log_a)) * (i * xb.astype(jnp.float32))
    hs = _linear_scan(a, b_t).reshape(bsz, seq, RNN_WIDTH)
    y = jax.nn.gelu(gate_branch, approximate=True) * hs.astype(h.dtype)
    return y @ w_out


def _chunk_gated_delta_rule(q, k, v, g, beta):
    bsz, seq, nh, dk = q.shape
    dv = v.shape[-1]
    cs = GDN_CHUNK
    n = seq // cs
    q = q.reshape(bsz, n, cs, nh, dk)
    k = k.reshape(bsz, n, cs, nh, dk)
    v = v.reshape(bsz, n, cs, nh, dv)
    beta = beta.reshape(bsz, n, cs, nh)
    gc = jnp.cumsum(g.reshape(bsz, n, cs, nh), axis=2)
    gc_t = jnp.swapaxes(gc, 2, 3)
    beta_t = jnp.swapaxes(beta, 2, 3)
    causal = jnp.tril(jnp.ones((cs, cs), dtype=bool))
    strict = jnp.tril(jnp.ones((cs, cs), dtype=bool), k=-1)
    decay = jnp.exp(jnp.where(causal, gc_t[..., :, None] - gc_t[..., None, :], -jnp.inf))
    a_kk = jnp.where(strict, jnp.einsum('bnihd,bnjhd->bnhij', k, k) * decay, 0.0) * beta_t[..., :, None]
    eye = jnp.eye(cs, dtype=jnp.float32)
    t_mat = lax.linalg.triangular_solve(eye + a_kk, jnp.broadcast_to(eye, a_kk.shape),
                                        left_side=True, lower=True, unit_diagonal=True)
    w = jnp.einsum('bnhij,bnjhd->nbhid', t_mat, k * (beta * jnp.exp(gc))[..., None])
    u = jnp.einsum('bnhij,bnjhd->nbhid', t_mat, v * beta[..., None])
    q_dec = jnp.transpose(q * jnp.exp(gc)[..., None], (1, 0, 3, 2, 4))
    a_qk = jnp.moveaxis(jnp.where(causal, jnp.einsum('bnihd,bnjhd->bnhij', q, k) * decay, 0.0), 1, 0)
    k_tail = jnp.transpose(k * jnp.exp(gc[:, :, -1:, :] - gc)[..., None], (1, 0, 3, 2, 4))
    chunk_decay = jnp.moveaxis(jnp.exp(gc[:, :, -1, :]), 1, 0)

    def step(state, xs):
        w_c, u_c, q_c, a_c, k_c, dec_c = xs
        v_new = u_c - jnp.einsum('bhcd,bhde->bhce', w_c, state)
        o_c = jnp.einsum('bhcd,bhde->bhce', q_c, state) + jnp.einsum('bhij,bhje->bhie', a_c, v_new)
        state = state * dec_c[..., None, None] + jnp.einsum('bhcd,bhce->bhde', k_c, v_new)
        return state, o_c

    state0 = jnp.zeros((bsz, nh, dk, dv), jnp.float32)
    _, o = lax.scan(step, state0, (w, u, q_dec, a_qk, k_tail, chunk_decay))
    return jnp.transpose(o, (1, 0, 3, 2, 4)).reshape(bsz, seq, nh, dv)


def gated_deltanet_mixer(h, w_in, conv_w, a_log, dt_bias, norm_w, w_out):
    bsz, seq, _ = h.shape
    qkv, z, b_raw, a_raw = jnp.split(
        h @ w_in, [GDN_CONV_DIM, GDN_CONV_DIM + GDN_VALUE_DIM, GDN_CONV_DIM + GDN_VALUE_DIM + GDN_V_HEADS], axis=-1)
    qkv = jax.nn.silu(_causal_depthwise_conv(qkv, conv_w)).astype(jnp.float32)
    q, k, v = jnp.split(qkv, [GDN_KEY_DIM, 2 * GDN_KEY_DIM], axis=-1)
    rep = GDN_V_HEADS // GDN_QK_HEADS
    q = jnp.repeat(_l2norm(q.reshape(bsz, seq, GDN_QK_HEADS, GDN_HEAD_DIM)) * GDN_HEAD_DIM ** -0.5, rep, axis=2)
    k = jnp.repeat(_l2norm(k.reshape(bsz, seq, GDN_QK_HEADS, GDN_HEAD_DIM)), rep, axis=2)
    v = v.reshape(bsz, seq, GDN_V_HEADS, GDN_HEAD_DIM)
    beta = jax.nn.sigmoid(b_raw.astype(jnp.float32))
    g = -jnp.exp(a_log.astype(jnp.float32)) * jax.nn.softplus(a_raw.astype(jnp.float32) + dt_bias.astype(jnp.float32))
    o = _chunk_gated_delta_rule(q, k, v, g, beta)
    o = _rms_norm_gated(o, norm_w, z.reshape(bsz, seq, GDN_V_HEADS, GDN_HEAD_DIM))
    return o.reshape(bsz, seq, GDN_VALUE_DIM).astype(h.dtype) @ w_out


def _chunk_gla(q, k, v, log_alpha):
    bsz, seq, nh, dk = q.shape
    dv = v.shape[-1]
    cs = GLA_CHUNK
    n = seq // cs
    q = q.reshape(bsz, n, cs, nh, dk)
    k = k.reshape(bsz, n, cs, nh, dk)
    v = v.reshape(bsz, n, cs, nh, dv)
    bcum = jnp.cumsum(log_alpha.reshape(bsz, n, cs, nh, dk), axis=2)
    q_dec = q * jnp.exp(bcum)
    k_inv = k * jnp.exp(-bcum)
    causal = jnp.tril(jnp.ones((cs, cs), dtype=bool))
    a_qk = jnp.where(causal, jnp.einsum('bnihd,bnjhd->bnhij', q_dec, k_inv), 0.0)
    intra = jnp.einsum('bnhij,bnjhe->bnihe', a_qk, v)
    k_tail = k * jnp.exp(bcum[:, :, -1:] - bcum)
    chunk_decay = jnp.exp(bcum[:, :, -1])

    def step(state, xs):
        q_c, k_c, v_c, dec_c = xs
        o_c = jnp.einsum('bchd,bhde->bche', q_c, state)
        state = state * dec_c[..., None] + jnp.einsum('bchd,bche->bhde', k_c, v_c)
        return state, o_c

    state0 = jnp.zeros((bsz, nh, dk, dv), jnp.float32)
    xs = (jnp.moveaxis(q_dec, 1, 0), jnp.moveaxis(k_tail, 1, 0), jnp.moveaxis(v, 1, 0), jnp.moveaxis(chunk_decay, 1, 0))
    _, inter = lax.scan(step, state0, xs)
    return (jnp.moveaxis(inter, 0, 1) + intra).reshape(bsz, seq, nh, dv)


def gla_mixer(h, w_in, w_alpha_up, b_alpha, norm_w, w_out):
    bsz, seq, _ = h.shape
    q, k, v, gate, a_low = jnp.split(
        h @ w_in, [GLA_KEY_DIM, 2 * GLA_KEY_DIM, 2 * GLA_KEY_DIM + GLA_VALUE_DIM, 2 * GLA_KEY_DIM + 2 * GLA_VALUE_DIM], axis=-1)
    log_alpha = jax.nn.log_sigmoid((a_low @ w_alpha_up + b_alpha).astype(jnp.float32)) / GLA_GATE_NORMALIZER
    q = q.astype(jnp.float32).reshape(bsz, seq, GLA_HEADS, GLA_HEAD_K) * GLA_HEAD_K ** -0.5
    k = k.astype(jnp.float32).reshape(bsz, seq, GLA_HEADS, GLA_HEAD_K)
    v = v.astype(jnp.float32).reshape(bsz, seq, GLA_HEADS, GLA_HEAD_V)
    o = _chunk_gla(q, k, v, log_alpha.reshape(bsz, seq, GLA_HEADS, GLA_HEAD_K))
    o = _rms_norm_gated(o, norm_w, gate.reshape(bsz, seq, GLA_HEADS, GLA_HEAD_V))
    return o.reshape(bsz, seq, GLA_VALUE_DIM).astype(h.dtype) @ w_out


def sq_relu_mlp(h, w1, w2):
    return jnp.square(jax.nn.relu(h @ w1)) @ w2


def _residual(x, shift, scale, gate, ln_g, ln_b, sublayer):
    h = x * (1.0 + scale[:, None, :]) + shift[:, None, :]
    return _layer_norm(DEEPNORM_ALPHA * x + (1.0 + gate[:, None, :]) * sublayer(h), ln_g, ln_b)


def setup_inputs(seed: int = 0) -> dict:
    key = jax.random.key(seed)
    keys = iter(jax.random.split(key, 40))
    f32 = jnp.float32

    def nrm(shape, scale):
        return jax.random.normal(next(keys), shape, f32) * scale

    def unif(shape, lo, hi):
        return jax.random.uniform(next(keys), shape, f32, minval=lo, maxval=hi)

    x = nrm((BATCH, SEQ, D_MODEL), 1.0)
    c = nrm((BATCH, D_MODEL), 1.0)
    ln_g = 1.0 + nrm((DEPTH, 2, D_MODEL), 0.02)
    ln_b = nrm((DEPTH, 2, D_MODEL), 0.02)
    w_mod = nrm((DEPTH, D_MODEL, N_MOD * D_MODEL), 0.25 * D_MODEL ** -0.5)
    b_mod = nrm((DEPTH, N_MOD * D_MODEL), 0.02)
    w_ff1 = nrm((DEPTH, D_MODEL, D_FF), D_MODEL ** -0.5)
    w_ff2 = nrm((DEPTH, D_FF, D_MODEL), DEEPNORM_BETA * D_FF ** -0.5)
    na = N_RGLRU_LAYERS
    rglru_w_in = nrm((na, D_MODEL, 2 * RNN_WIDTH), D_MODEL ** -0.5)
    rglru_conv_w = nrm((na, CONV_WIDTH, RNN_WIDTH), CONV_WIDTH ** -0.5)
    rglru_conv_b = nrm((na, RNN_WIDTH), 0.02)
    rglru_w_rgate = nrm((na, RNN_BLOCKS, RNN_BLOCK_DIM, RNN_BLOCK_DIM), RNN_BLOCK_DIM ** -0.5)
    rglru_b_rgate = nrm((na, RNN_BLOCKS, RNN_BLOCK_DIM), 0.02)
    rglru_w_igate = nrm((na, RNN_BLOCKS, RNN_BLOCK_DIM, RNN_BLOCK_DIM), RNN_BLOCK_DIM ** -0.5)
    rglru_b_igate = nrm((na, RNN_BLOCKS, RNN_BLOCK_DIM), 0.02)
    p = unif((na, RNN_WIDTH), 0.9, 0.999) ** (1.0 / LRU_C)
    rglru_lambda = jnp.log(p) - jnp.log1p(-p)
    rglru_w_out = nrm((na, RNN_WIDTH, D_MODEL), DEEPNORM_BETA * RNN_WIDTH ** -0.5)
    nb = N_GDN_LAYERS
    gdn_w_in = nrm((nb, D_MODEL, GDN_IN_DIM), D_MODEL ** -0.5)
    gdn_conv_w = nrm((nb, CONV_WIDTH, GDN_CONV_DIM), CONV_WIDTH ** -0.5)
    gdn_a_log = jnp.log(unif((nb, GDN_V_HEADS), 1.0, 16.0))
    dt = jnp.exp(unif((nb, GDN_V_HEADS), math.log(1e-3), math.log(1e-1)))
    gdn_dt_bias = dt + jnp.log(-jnp.expm1(-dt))
    gdn_norm_w = 1.0 + nrm((nb, GDN_HEAD_DIM), 0.02)
    gdn_w_out = nrm((nb, GDN_VALUE_DIM, D_MODEL), DEEPNORM_BETA * GDN_VALUE_DIM ** -0.5)
    nc = N_GLA_LAYERS
    gla_w_in = nrm((nc, D_MODEL, GLA_IN_DIM), D_MODEL ** -0.5)
    gla_w_alpha_up = nrm((nc, GLA_GATE_RANK, GLA_KEY_DIM), GLA_GATE_RANK ** -0.5)
    gla_b_alpha = nrm((nc, GLA_KEY_DIM), 0.1)
    gla_norm_w = 1.0 + nrm((nc, GLA_HEAD_V), 0.02)
    gla_w_out = nrm((nc, GLA_VALUE_DIM, D_MODEL), DEEPNORM_BETA * GLA_VALUE_DIM ** -0.5)
    return {
        'x': x, 'c': c, 'ln_g': ln_g, 'ln_b': ln_b, 'w_mod': w_mod, 'b_mod': b_mod,
        'w_ff1': w_ff1, 'w_ff2': w_ff2,
        'rglru_w_in': rglru_w_in, 'rglru_conv_w': rglru_conv_w, 'rglru_conv_b': rglru_conv_b,
        'rglru_w_rgate': rglru_w_rgate, 'rglru_b_rgate': rglru_b_rgate,
        'rglru_w_igate': rglru_w_igate, 'rglru_b_igate': rglru_b_igate,
        'rglru_lambda': rglru_lambda, 'rglru_w_out': rglru_w_out,
        'gdn_w_in': gdn_w_in, 'gdn_conv_w': gdn_conv_w, 'gdn_a_log': gdn_a_log,
        'gdn_dt_bias': gdn_dt_bias, 'gdn_norm_w': gdn_norm_w, 'gdn_w_out': gdn_w_out,
        'gla_w_in': gla_w_in, 'gla_w_alpha_up': gla_w_alpha_up, 'gla_b_alpha': gla_b_alpha,
        'gla_norm_w': gla_norm_w, 'gla_w_out': gla_w_out,
    }


def reference(x, c, ln_g, ln_b, w_mod, b_mod, w_ff1, w_ff2,
              rglru_w_in, rglru_conv_w, rglru_conv_b, rglru_w_rgate, rglru_b_rgate,
              rglru_w_igate, rglru_b_igate, rglru_lambda, rglru_w_out,
              gdn_w_in, gdn_conv_w, gdn_a_log, gdn_dt_bias, gdn_norm_w, gdn_w_out,
              gla_w_in, gla_w_alpha_up, gla_b_alpha, gla_norm_w, gla_w_out):
    c_act = jax.nn.silu(c)
    for i in range(DEPTH):
        mod = c_act @ w_mod[i] + b_mod[i]
        sh_m, sc_m, gt_m, sh_f, sc_f, gt_f = jnp.split(mod, N_MOD, axis=-1)
        kind, slot = i % N_MIXERS, i // N_MIXERS
        if kind == 0:
            mixer = functools.partial(
                rglru_mixer, w_in=rglru_w_in[slot], conv_w=rglru_conv_w[slot], conv_b=rglru_conv_b[slot],
                w_rgate=rglru_w_rgate[slot], b_rgate=rglru_b_rgate[slot], w_igate=rglru_w_igate[slot],
                b_igate=rglru_b_igate[slot], lam=rglru_lambda[slot], w_out=rglru_w_out[slot])
        elif kind == 1:
            mixer = functools.partial(
                gated_deltanet_mixer, w_in=gdn_w_in[slot], conv_w=gdn_conv_w[slot], a_log=gdn_a_log[slot],
                dt_bias=gdn_dt_bias[slot], norm_w=gdn_norm_w[slot], w_out=gdn_w_out[slot])
        else:
            mixer = functools.partial(
                gla_mixer, w_in=gla_w_in[slot], w_alpha_up=gla_w_alpha_up[slot], b_alpha=gla_b_alpha[slot],
                norm_w=gla_norm_w[slot], w_out=gla_w_out[slot])
        x = _residual(x, sh_m, sc_m, gt_m, ln_g[i, 0], ln_b[i, 0], mixer)
        x = _residual(x, sh_f, sc_f, gt_f, ln_g[i, 1], ln_b[i, 1],
                      functools.partial(sq_relu_mlp, w1=w_ff1[i], w2=w_ff2[i]))
    return x
```

```python
import functools

import jax
import jax.numpy as jnp
from jax import lax
from jax.experimental import pallas as pl
from jax.experimental.pallas import tpu as pltpu

F32 = jnp.float32
BF16 = jnp.bfloat16

D_MODEL = 2048
DEPTH = 4
N_MIXERS = 3
DEEPNORM_ALPHA = float((2 * DEPTH) ** 0.25)
LN_EPS = 1e-5
RMS_EPS = 1e-6
D_FF = 4 * D_MODEL
N_MOD = 6
CONV_WIDTH = 4

RNN_WIDTH = (5 * D_MODEL) // 4
RNN_BLOCKS = 16
RNN_BLOCK_DIM = RNN_WIDTH // RNN_BLOCKS
RNN_GROUP = 640
RNN_GROUPS = RNN_WIDTH // RNN_GROUP
LRU_C = 8.0

GDN_HEAD_DIM = 128
GDN_QK_HEADS = D_MODEL // GDN_HEAD_DIM
GDN_V_HEADS = 2 * GDN_QK_HEADS
GDN_KEY_DIM = GDN_QK_HEADS * GDN_HEAD_DIM
GDN_VALUE_DIM = GDN_V_HEADS * GDN_HEAD_DIM
GDN_CONV_DIM = 2 * GDN_KEY_DIM + GDN_VALUE_DIM
GDN_PAIRS = GDN_QK_HEADS

GLA_HEADS = 4
GLA_KEY_DIM = D_MODEL // 2
GLA_VALUE_DIM = D_MODEL
GLA_HEAD_K = GLA_KEY_DIM // GLA_HEADS
GLA_HEAD_V = GLA_VALUE_DIM // GLA_HEADS
GLA_GATE_RANK = 16
GLA_GATE_NORMALIZER = 16.0

CHUNK = 64
LANES = 128
SUBLANES = 8
PROJ_TN = 1280
VMEM_LIMIT = 56 * 1024 * 1024


def _dot(a, b):
    return jnp.dot(a, b, preferred_element_type=F32)


def _dot_nt(a, b):
    return lax.dot_general(a, b, (((1,), (1,)), ((), ())), preferred_element_type=F32)


def _dot_tn(a, b):
    return lax.dot_general(a, b, (((0,), (0,)), ((), ())), preferred_element_type=F32)


def _bf(x):
    return x.astype(BF16)


def _split3(x):
    hi = _bf(x)
    r1 = x - hi.astype(F32)
    mid = _bf(r1)
    lo = _bf(r1 - mid.astype(F32))
    return hi, mid, lo


def _dot_mask_lhs(m, x):
    hi, mid, lo = _split3(x)
    return _dot(m, hi) + _dot(m, mid) + _dot(m, lo)


def _dot_mask_rhs(x, m):
    hi, mid, lo = _split3(x)
    return _dot(hi, m) + _dot(mid, m) + _dot(lo, m)


def _softplus(x):
    return jnp.maximum(x, 0.0) + jnp.log1p(jnp.exp(-jnp.abs(x)))


def _silu(x):
    return x * jax.nn.sigmoid(x)


def _params(*sem):
    return pltpu.CompilerParams(dimension_semantics=sem, vmem_limit_bytes=VMEM_LIMIT)


def _layer_norm_rows(r, g, b):
    mu = jnp.mean(r, axis=-1, keepdims=True)
    d = r - mu
    var = jnp.mean(d * d, axis=-1, keepdims=True)
    return d * lax.rsqrt(var + LN_EPS) * g + b


def _mod_kernel(c_ref, w_ref, b_ref, o_ref):
    c = c_ref[...]
    o_ref[0] = _dot(_bf(_silu(c)), _bf(w_ref[0])) + b_ref[0]


def _mod_call(c_pad, w_mod, b_mod):
    depth, d, n = w_mod.shape
    rows = c_pad.shape[0]
    tn = 1024
    return pl.pallas_call(
        _mod_kernel,
        grid=(depth, n // tn),
        in_specs=[
            pl.BlockSpec((rows, d), lambda l, j: (0, 0)),
            pl.BlockSpec((1, d, tn), lambda l, j: (l, 0, j)),
            pl.BlockSpec((1, 1, tn), lambda l, j: (l, 0, j)),
        ],
        out_specs=pl.BlockSpec((1, rows, tn), lambda l, j: (l, 0, j)),
        out_shape=jax.ShapeDtypeStruct((depth, rows, n), F32),
        compiler_params=_params("arbitrary", "arbitrary"),
        name="mod",
    )(c_pad, w_mod, b_mod.reshape(depth, 1, n))


def _proj_kernel(x_ref, sh_ref, sc_ref, w_ref, o_ref, h_scr):
    @pl.when(pl.program_id(1) == 0)
    def _():
        h_scr[...] = _bf(x_ref[...] * (1.0 + sc_ref[0]) + sh_ref[0])

    o_ref[...] = _dot(h_scr[...], w_ref[...])


def _proj_call(x, shift, scale, w, seq):
    n, d = x.shape
    nout = w.shape[1]
    tm = min(1024, seq)
    tn = PROJ_TN
    tpb = seq // tm
    return pl.pallas_call(
        _proj_kernel,
        grid=(n // tm, nout // tn),
        in_specs=[
            pl.BlockSpec((tm, d), lambda i, j: (i, 0)),
            pl.BlockSpec((1, 1, d), lambda i, j: (i // tpb, 0, 0)),
            pl.BlockSpec((1, 1, d), lambda i, j: (i // tpb, 0, 0)),
            pl.BlockSpec((d, tn), lambda i, j: (0, j)),
        ],
        out_specs=pl.BlockSpec((tm, tn), lambda i, j: (i, j)),
        out_shape=jax.ShapeDtypeStruct((n, nout), F32),
        scratch_shapes=[pltpu.VMEM((tm, d), BF16)],
        compiler_params=_params("arbitrary", "arbitrary"),
        name="proj",
    )(x, shift, scale, w)


def _outln_kernel(y_ref, w_ref, x_ref, gt_ref, g_ref, b_ref, o_ref):
    k = pl.program_id(1)

    @pl.when(k == 0)
    def _():
        o_ref[...] = jnp.zeros_like(o_ref)

    o_ref[...] += _dot(y_ref[...], w_ref[...])

    @pl.when(k == pl.num_programs(1) - 1)
    def _():
        r = DEEPNORM_ALPHA * x_ref[...] + (1.0 + gt_ref[0]) * o_ref[...]
        o_ref[...] = _layer_norm_rows(r, g_ref[...], b_ref[...])


def _outln_call(y, w, x, gate, ln_g, ln_b, seq):
    n, d = x.shape
    kdim = y.shape[1]
    tm = min(512, seq)
    tk = 512
    tpb = seq // tm
    return pl.pallas_call(
        _outln_kernel,
        grid=(n // tm, kdim // tk),
        in_specs=[
            pl.BlockSpec((tm, tk), lambda i, k: (i, k)),
            pl.BlockSpec((tk, d), lambda i, k: (k, 0)),
            pl.BlockSpec((tm, d), lambda i, k: (i, 0)),
            pl.BlockSpec((1, 1, d), lambda i, k: (i // tpb, 0, 0)),
            pl.BlockSpec((1, d), lambda i, k: (0, 0)),
            pl.BlockSpec((1, d), lambda i, k: (0, 0)),
        ],
        out_specs=pl.BlockSpec((tm, d), lambda i, k: (i, 0)),
        out_shape=jax.ShapeDtypeStruct((n, d), F32),
        compiler_params=_params("arbitrary", "arbitrary"),
        name="outln",
    )(y, w, x, gate, ln_g, ln_b)


def _mlp_kernel(x_ref, sh_ref, sc_ref, gt_ref, w1_ref, w2_ref, g_ref, b_ref, o_ref, h_scr):
    k = pl.program_id(1)

    @pl.when(k == 0)
    def _():
        h_scr[...] = _bf(x_ref[...] * (1.0 + sc_ref[0]) + sh_ref[0])
        o_ref[...] = jnp.zeros_like(o_ref)

    hid = jnp.maximum(_dot(h_scr[...], w1_ref[...]), 0.0)
    o_ref[...] += _dot(_bf(hid * hid), w2_ref[...])

    @pl.when(k == pl.num_programs(1) - 1)
    def _():
        r = DEEPNORM_ALPHA * x_ref[...] + (1.0 + gt_ref[0]) * o_ref[...]
        o_ref[...] = _layer_norm_rows(r, g_ref[...], b_ref[...])


def _mlp_call(x, shift, scale, gate, w1, w2, ln_g, ln_b, seq):
    n, d = x.shape
    dff = w1.shape[1]
    tm = min(512, seq)
    tf = 1024
    tpb = seq // tm
    vec = pl.BlockSpec((1, 1, d), lambda i, k: (i // tpb, 0, 0))
    return pl.pallas_call(
        _mlp_kernel,
        grid=(n // tm, dff // tf),
        in_specs=[
            pl.BlockSpec((tm, d), lambda i, k: (i, 0)),
            vec, vec, vec,
            pl.BlockSpec((d, tf), lambda i, k: (0, k)),
            pl.BlockSpec((tf, d), lambda i, k: (k, 0)),
            pl.BlockSpec((1, d), lambda i, k: (0, 0)),
            pl.BlockSpec((1, d), lambda i, k: (0, 0)),
        ],
        out_specs=pl.BlockSpec((tm, d), lambda i, k: (i, 0)),
        out_shape=jax.ShapeDtypeStruct((n, d), F32),
        scratch_shapes=[pltpu.VMEM((tm, d), BF16)],
        compiler_params=_params("arbitrary", "arbitrary"),
        name="mlp",
    )(x, shift, scale, gate, w1, w2, ln_g, ln_b)


def _rglru_kernel(gate_ref, rec_ref, cw_ref, cb_ref, wg_ref, bg_ref, lam_ref, o_ref,
                  xbuf, abuf, bbuf, hcar):
    tm = rec_ref.shape[0]

    @pl.when(pl.program_id(1) == 0)
    def _():
        xbuf[0:SUBLANES, :] = jnp.zeros((SUBLANES, RNN_WIDTH), F32)
        hcar[...] = jnp.zeros_like(hcar)

    xbuf[SUBLANES:SUBLANES + tm, :] = rec_ref[...]
    sp = _softplus(-lam_ref[...])
    for g in range(RNN_GROUPS):
        cs = slice(RNN_GROUP * g, RNN_GROUP * (g + 1))
        xr = cb_ref[:, cs]
        for j in range(CONV_WIDTH):
            r0 = SUBLANES - (CONV_WIDTH - 1) + j
            xr = xr + cw_ref[j:j + 1, cs] * xbuf[r0:r0 + tm, cs]
        pre = _dot(_bf(xr), wg_ref[g]) + bg_ref[g]
        r_gate = jax.nn.sigmoid(pre[:, :RNN_GROUP])
        i_gate = jax.nn.sigmoid(pre[:, RNN_GROUP:])
        log_a = (-LRU_C) * r_gate * sp[:, cs]
        a = jnp.exp(log_a)
        abuf[:, cs] = a
        bbuf[:, cs] = jnp.sqrt(-jnp.tanh(log_a) * (1.0 + a * a)) * (i_gate * xr)
    xbuf[0:SUBLANES, :] = xbuf[tm:tm + SUBLANES, :]

    row = lax.broadcasted_iota(jnp.int32, (SUBLANES, RNN_WIDTH), 0)

    def body(j, carry):
        r0 = pl.multiple_of(j * SUBLANES, SUBLANES)
        a = abuf[pl.ds(r0, SUBLANES), :]
        b = bbuf[pl.ds(r0, SUBLANES), :]
        for k in (1, 2, 4):
            a_s = jnp.where(row >= k, pltpu.roll(a, k, 0), 1.0)
            b_s = jnp.where(row >= k, pltpu.roll(b, k, 0), 0.0)
            b = a * b_s + b
            a = a * a_s
        h = a * carry + b
        bbuf[pl.ds(r0, SUBLANES), :] = h
        return h[SUBLANES - 1:SUBLANES, :]

    hcar[...] = lax.fori_loop(0, tm // SUBLANES, body, hcar[...])
    o_ref[...] = _bf(jax.nn.gelu(gate_ref[...], approximate=True) * bbuf[...])


def _rglru_core_call(u, conv_w, conv_b, wg, bg, lam, batch, seq):
    n = u.shape[0]
    tm = min(256, seq)
    tpb = seq // tm
    r = RNN_WIDTH
    const2 = lambda b, t: (0, 0)
    return pl.pallas_call(
        _rglru_kernel,
        grid=(batch, tpb),
        in_specs=[
            pl.BlockSpec((tm, r), lambda b, t: (b * tpb + t, 0)),
            pl.BlockSpec((tm, r), lambda b, t: (b * tpb + t, 1)),
            pl.BlockSpec((CONV_WIDTH, r), const2),
            pl.BlockSpec((1, r), const2),
            pl.BlockSpec((RNN_GROUPS, RNN_GROUP, 2 * RNN_GROUP), lambda b, t: (0, 0, 0)),
            pl.BlockSpec((RNN_GROUPS, 1, 2 * RNN_GROUP), lambda b, t: (0, 0, 0)),
            pl.BlockSpec((1, r), const2),
        ],
        out_specs=pl.BlockSpec((tm, r), lambda b, t: (b * tpb + t, 0)),
        out_shape=jax.ShapeDtypeStruct((n, r), BF16),
        scratch_shapes=[
            pltpu.VMEM((tm + SUBLANES, r), F32),
            pltpu.VMEM((tm, r), F32),
            pltpu.VMEM((tm, r), F32),
            pltpu.VMEM((1, r), F32),
        ],
        compiler_params=_params("arbitrary", "arbitrary"),
        name="rglru_core",
    )(u, u, conv_w, conv_b, wg, bg, lam)


def _gdn_kernel(qkv_ref, z_ref, br_ref, ar_ref, cw_ref, alog_ref, dtb_ref, nw_ref, o_ref,
                xbuf, qs, ks, vs, gcb, btb, gct, s_scr, o_scr):
    c = CHUNK
    hd = GDN_HEAD_DIM

    @pl.when(pl.program_id(1) == 0)
    def _():
        xbuf[0:SUBLANES, :] = jnp.zeros((SUBLANES, GDN_CONV_DIM), F32)
        s_scr[...] = jnp.zeros_like(s_scr)

    xbuf[SUBLANES:SUBLANES + c, :] = qkv_ref[...]

    beta = jax.nn.sigmoid(br_ref[...])
    g = -jnp.exp(alog_ref[...]) * _softplus(ar_ref[...] + dtb_ref[...])
    ti64 = lax.broadcasted_iota(jnp.int32, (c, c), 0)
    tj64 = lax.broadcasted_iota(jnp.int32, (c, c), 1)
    tri = jnp.where(ti64 >= tj64, 1.0, 0.0).astype(BF16)
    gc = _dot_mask_lhs(tri, g)
    g_t = jnp.concatenate([g, jnp.zeros_like(g)], axis=0).T
    ut = lax.broadcasted_iota(jnp.int32, (2 * c, 2 * c), 0)
    uj = lax.broadcasted_iota(jnp.int32, (2 * c, 2 * c), 1) & (c - 1)
    upper2 = jnp.where((ut <= uj) & (ut < c), 1.0, 0.0).astype(BF16)
    gct[...] = _dot_mask_rhs(g_t, upper2)
    for h in range(GDN_V_HEADS):
        gcb[h] = jnp.broadcast_to(gc[:, h:h + 1], (c, LANES))
        btb[h] = jnp.broadcast_to(beta[:, h:h + 1], (c, LANES))

    def conv_silu(col0, width):
        acc = None
        for j in range(CONV_WIDTH):
            r0 = SUBLANES - (CONV_WIDTH - 1) + j
            term = cw_ref[j:j + 1, col0:col0 + width] * xbuf[r0:r0 + c, col0:col0 + width]
            acc = term if acc is None else acc + term
        return _silu(acc)

    for p in range(GDN_PAIRS):
        q = conv_silu(hd * p, hd)
        q = q * lax.rsqrt(jnp.sum(q * q, axis=-1, keepdims=True) + RMS_EPS) * (hd ** -0.5)
        k = conv_silu(GDN_KEY_DIM + hd * p, hd)
        k = k * lax.rsqrt(jnp.sum(k * k, axis=-1, keepdims=True) + RMS_EPS)
        qs[p] = q
        ks[p] = k
        vs[p] = conv_silu(2 * GDN_KEY_DIM + 2 * hd * p, 2 * hd)
    xbuf[0:SUBLANES, :] = xbuf[c:c + SUBLANES, :]

    ti = lax.broadcasted_iota(jnp.int32, (c, LANES), 0)
    lane = lax.broadcasted_iota(jnp.int32, (c, LANES), 1)
    left = lane < c
    tj = lane & (c - 1)
    left_row = lax.broadcasted_iota(jnp.int32, (1, LANES), 1) < c
    eye = jnp.where(ti == tj, 1.0, 0.0)
    left2 = lax.broadcasted_iota(jnp.int32, (c, 2 * hd), 1) < hd
    left2s = lax.broadcasted_iota(jnp.int32, (hd, 2 * hd), 1) < hd
    zero_tile = jnp.zeros((c, hd), F32)

    def block_diag(x, msk):
        return jnp.concatenate([jnp.where(msk, x, 0.0), jnp.where(msk, 0.0, x)], axis=0)

    def pair_body(p, carry):
        h0 = 2 * p
        h1 = 2 * p + 1
        q = qs[p]
        k = ks[p]
        v_ss = vs[p]
        gcb0 = gcb[h0]
        gcb1 = gcb[h1]
        bt0 = btb[h0]
        bt1 = btb[h1]
        col = jnp.where(left, gcb0, gcb1)
        row = jnp.where(left_row, gct[pl.ds(h0, 1), :], gct[pl.ds(h1, 1), :])
        decay = jnp.where(ti >= tj, jnp.exp(jnp.minimum(col - row, 0.0)), 0.0)
        kk2 = jnp.concatenate([k, k], axis=0)
        kq = _dot_nt(_bf(jnp.concatenate([k, q], axis=0)), _bf(kk2))
        a_kk = jnp.where(ti > tj, kq[:c] * decay, 0.0) * jnp.where(left, bt0, bt1)
        a_qk = kq[c:] * decay

        w_pow = _dot(_bf(a_kk), _bf(block_diag(a_kk, left)))
        t_inv = eye - a_kk
        for _ in range(4):
            wb = _bf(block_diag(w_pow, left))
            r = _dot(_bf(jnp.concatenate([w_pow, t_inv], axis=0)), wb)
            w_pow = r[:c]
            t_inv = t_inv + r[c:]
        t_inv = t_inv + _dot(_bf(t_inv), _bf(block_diag(w_pow, left)))

        eg0 = jnp.exp(gcb0)
        eg1 = jnp.exp(gcb1)
        v0 = v_ss[:, :hd]
        v1 = v_ss[:, hd:]
        rhs = jnp.concatenate([
            jnp.concatenate([k * (bt0 * eg0), zero_tile, v0 * bt0, zero_tile], axis=1),
            jnp.concatenate([zero_tile, k * (bt1 * eg1), zero_tile, v1 * bt1], axis=1)], axis=0)
        wu = _dot(_bf(t_inv), _bf(rhs))
        q_dec = jnp.concatenate([q * eg0, q * eg1], axis=1)
        s_ss = s_scr[p]
        r1 = _dot(_bf(jnp.concatenate([wu[:, :2 * hd], q_dec], axis=0)),
                  _bf(block_diag(s_ss, left2s)))
        v_new = wu[:, 2 * hd:] - r1[:c]
        gl0 = gcb0[c - 1:c, :]
        gl1 = gcb1[c - 1:c, :]
        k_tail = kk2.T * jnp.exp(jnp.where(left_row, gl0, gl1) - row)
        r2 = _dot(_bf(jnp.concatenate([a_qk, k_tail], axis=0)),
                  _bf(block_diag(v_new, left2)))
        o_scr[p] = r1[c:] + r2[:c]
        s_scr[p] = s_ss * jnp.concatenate([jnp.exp(gl0), jnp.exp(gl1)], axis=1) + r2[c:]
        return carry

    lax.fori_loop(0, GDN_PAIRS, pair_body, 0)

    nw = nw_ref[...]
    for p in range(GDN_PAIRS):
        o_ss = o_scr[p]
        for s in range(2):
            cs = slice(2 * hd * p + hd * s, 2 * hd * p + hd * (s + 1))
            o = o_ss[:, hd * s:hd * (s + 1)]
            nrm = o * lax.rsqrt(jnp.mean(o * o, axis=-1, keepdims=True) + RMS_EPS)
            o_ref[:, cs] = _bf(nrm * nw * _silu(z_ref[:, cs]))


def _gdn_core_call(u, conv_w, a_log, dt_bias, norm_w, batch, seq):
    n = u.shape[0]
    c = CHUNK
    cpb = seq // c
    hd = GDN_HEAD_DIM
    row = lambda b, t: b * cpb + t
    const2 = lambda b, t: (0, 0)
    gate_blk = (GDN_CONV_DIM + GDN_VALUE_DIM) // LANES
    return pl.pallas_call(
        _gdn_kernel,
        grid=(batch, cpb),
        in_specs=[
            pl.BlockSpec((c, GDN_CONV_DIM), lambda b, t: (row(b, t), 0)),
            pl.BlockSpec((c, GDN_VALUE_DIM), lambda b, t: (row(b, t), GDN_CONV_DIM // GDN_VALUE_DIM)),
            pl.BlockSpec((c, LANES), lambda b, t: (row(b, t), gate_blk)),
            pl.BlockSpec((c, LANES), lambda b, t: (row(b, t), gate_blk + 1)),
            pl.BlockSpec((CONV_WIDTH, GDN_CONV_DIM), const2),
            pl.BlockSpec((1, LANES), const2),
            pl.BlockSpec((1, LANES), const2),
            pl.BlockSpec((1, hd), const2),
        ],
        out_specs=pl.BlockSpec((c, GDN_VALUE_DIM), lambda b, t: (row(b, t), 0)),
        out_shape=jax.ShapeDtypeStruct((n, GDN_VALUE_DIM), BF16),
        scratch_shapes=[
            pltpu.VMEM((c + SUBLANES, GDN_CONV_DIM), F32),
            pltpu.VMEM((GDN_PAIRS, c, hd), F32),
            pltpu.VMEM((GDN_PAIRS, c, hd), F32),
            pltpu.VMEM((GDN_PAIRS, c, 2 * hd), F32),
            pltpu.VMEM((GDN_V_HEADS, c, LANES), F32),
            pltpu.VMEM((GDN_V_HEADS, c, LANES), F32),
            pltpu.VMEM((2 * c, 2 * c), F32),
            pltpu.VMEM((GDN_PAIRS, hd, 2 * hd), F32),
            pltpu.VMEM((GDN_PAIRS, c, 2 * hd), F32),
        ],
        compiler_params=_params("arbitrary", "arbitrary"),
        name="gdn_core",
    )(u, u, u, u, conv_w, a_log, dt_bias, norm_w)


def _gla_kernel(q_ref, k_ref, v_ref, z_ref, al_ref, wup_ref, bal_ref, nw_ref, o_ref, s_scr):
    c = CHUNK
    dk = GLA_HEAD_K
    dv = GLA_HEAD_V

    @pl.when(pl.program_id(1) == 0)
    def _():
        s_scr[...] = jnp.zeros_like(s_scr)

    a_low = _bf(al_ref[...])
    ti = lax.broadcasted_iota(jnp.int32, (c, c), 0)
    tj = lax.broadcasted_iota(jnp.int32, (c, c), 1)
    causal = ti >= tj
    tri = jnp.where(causal, 1.0, 0.0).astype(BF16)
    nw = nw_ref[...]
    for h in range(GLA_HEADS):
        ks_ = slice(dk * h, dk * (h + 1))
        vs_ = slice(dv * h, dv * (h + 1))
        pre = _dot(a_low, wup_ref[:, ks_]) + bal_ref[:, ks_]
        log_alpha = (-1.0 / GLA_GATE_NORMALIZER) * _softplus(-pre)
        bcum = _dot_mask_lhs(tri, log_alpha)
        q = q_ref[:, ks_] * (dk ** -0.5)
        k = k_ref[:, ks_]
        v = _bf(v_ref[:, vs_])
        q_dec = _bf(q * jnp.exp(bcum))
        k_inv = _bf(k * jnp.exp(-bcum))
        a_qk = jnp.where(causal, _dot_nt(q_dec, k_inv), 0.0)
        s_t = s_scr[h]
        o = _dot_nt(q_dec, _bf(s_t)) + _dot(_bf(a_qk), v)
        b_last = bcum[c - 1:c, :]
        k_tail = _bf(k * jnp.exp(b_last - bcum))
        s_scr[h] = s_t * jnp.exp(b_last) + _dot_tn(v, k_tail)
        nrm = o * lax.rsqrt(jnp.mean(o * o, axis=-1, keepdims=True) + RMS_EPS)
        o_ref[:, vs_] = _bf(nrm * nw * _silu(z_ref[:, vs_]))


def _gla_core_call(u, w_up, b_alpha, norm_w, batch, seq):
    n = u.shape[0]
    c = CHUNK
    cpb = seq // c
    row = lambda b, t: b * cpb + t
    const2 = lambda b, t: (0, 0)
    return pl.pallas_call(
        _gla_kernel,
        grid=(batch, cpb),
        in_specs=[
            pl.BlockSpec((c, GLA_KEY_DIM), lambda b, t: (row(b, t), 0)),
            pl.BlockSpec((c, GLA_KEY_DIM), lambda b, t: (row(b, t), 1)),
            pl.BlockSpec((c, GLA_VALUE_DIM), lambda b, t: (row(b, t), 1)),
            pl.BlockSpec((c, GLA_VALUE_DIM), lambda b, t: (row(b, t), 2)),
            pl.BlockSpec((c, LANES), lambda b, t: (row(b, t), (2 * GLA_KEY_DIM + 2 * GLA_VALUE_DIM) // LANES)),
            pl.BlockSpec((LANES, GLA_KEY_DIM), const2),
            pl.BlockSpec((1, GLA_KEY_DIM), const2),
            pl.BlockSpec((1, GLA_HEAD_V), const2),
        ],
        out_specs=pl.BlockSpec((c, GLA_VALUE_DIM), lambda b, t: (row(b, t), 0)),
        out_shape=jax.ShapeDtypeStruct((n, GLA_VALUE_DIM), BF16),
        scratch_shapes=[pltpu.VMEM((GLA_HEADS, GLA_HEAD_V, GLA_HEAD_K), F32)],
        compiler_params=_params("arbitrary", "arbitrary"),
        name="gla_core",
    )(u, u, u, u, u, w_up, b_alpha, norm_w)


def _pad_cols(w, total):
    return jnp.pad(w, ((0, 0), (0, total - w.shape[1])))


def _rglru_gate_weights(w_rgate, b_rgate, w_igate, b_igate):
    per = RNN_GROUP // RNN_BLOCK_DIM
    eye = jnp.eye(per, dtype=F32)

    def dense(w):
        w = w.reshape(RNN_GROUPS, per, RNN_BLOCK_DIM, RNN_BLOCK_DIM)
        return jnp.einsum("gjde,jk->gjdke", w, eye).reshape(RNN_GROUPS, RNN_GROUP, RNN_GROUP)

    wg = jnp.concatenate([dense(w_rgate), dense(w_igate)], axis=-1).astype(BF16)
    bg = jnp.concatenate([b_rgate.reshape(RNN_GROUPS, 1, RNN_GROUP),
                          b_igate.reshape(RNN_GROUPS, 1, RNN_GROUP)], axis=-1)
    return wg, bg


def _gdn_in_weights(w_in):
    main = GDN_CONV_DIM + GDN_VALUE_DIM
    b_w = _pad_cols(w_in[:, main:main + GDN_V_HEADS], LANES)
    a_w = _pad_cols(w_in[:, main + GDN_V_HEADS:main + 2 * GDN_V_HEADS], LANES)
    w = jnp.concatenate([w_in[:, :main], b_w, a_w], axis=1)
    total = -(-w.shape[1] // PROJ_TN) * PROJ_TN
    return _pad_cols(w, total).astype(BF16)


def _gla_in_weights(w_in):
    total = -(-(w_in.shape[1] - GLA_GATE_RANK + LANES) // PROJ_TN) * PROJ_TN
    return _pad_cols(w_in, total).astype(BF16)


def kernel(x, c, ln_g, ln_b, w_mod, b_mod, w_ff1, w_ff2, rglru_w_in, rglru_conv_w, rglru_conv_b, rglru_w_rgate, rglru_b_rgate, rglru_w_igate, rglru_b_igate, rglru_lambda, rglru_w_out, gdn_w_in, gdn_conv_w, gdn_a_log, gdn_dt_bias, gdn_norm_w, gdn_w_out, gla_w_in, gla_w_alpha_up, gla_b_alpha, gla_norm_w, gla_w_out):
    batch, seq, d = x.shape
    depth = w_mod.shape[0]
    n = batch * seq
    xf = x.reshape(n, d)

    c_pad = jnp.pad(c, ((0, SUBLANES - batch % SUBLANES if batch % SUBLANES else 0), (0, 0)))
    mod = _mod_call(c_pad, w_mod, b_mod)[:, :batch]
    mod = mod.reshape(depth, batch, N_MOD, 1, d)

    for i in range(depth):
        sh_m, sc_m, gt_m, sh_f, sc_f, gt_f = (mod[i, :, j] for j in range(N_MOD))
        kind, slot = i % N_MIXERS, i // N_MIXERS
        if kind == 0:
            u = _proj_call(xf, sh_m, sc_m, rglru_w_in[slot].astype(BF16), seq)
            wg, bg = _rglru_gate_weights(rglru_w_rgate[slot], rglru_b_rgate[slot],
                                         rglru_w_igate[slot], rglru_b_igate[slot])
            y = _rglru_core_call(u, rglru_conv_w[slot], rglru_conv_b[slot][None], wg, bg,
                                 rglru_lambda[slot][None], batch, seq)
            w_out = rglru_w_out[slot]
        elif kind == 1:
            u = _proj_call(xf, sh_m, sc_m, _gdn_in_weights(gdn_w_in[slot]), seq)
            y = _gdn_core_call(u, gdn_conv_w[slot],
                               _pad_cols(gdn_a_log[slot][None], LANES),
                               _pad_cols(gdn_dt_bias[slot][None], LANES),
                               gdn_norm_w[slot][None], batch, seq)
            w_out = gdn_w_out[slot]
        else:
            u = _proj_call(xf, sh_m, sc_m, _gla_in_weights(gla_w_in[slot]), seq)
            w_up = jnp.pad(gla_w_alpha_up[slot], ((0, LANES - GLA_GATE_RANK), (0, 0))).astype(BF16)
            y = _gla_core_call(u, w_up, gla_b_alpha[slot][None], gla_norm_w[slot][None], batch, seq)
            w_out = gla_w_out[slot]
        xf = _outln_call(y, w_out.astype(BF16), xf, gt_m, ln_g[i, 0][None], ln_b[i, 0][None], seq)
        xf = _mlp_call(xf, sh_f, sc_f, gt_f, w_ff1[i].astype(BF16), w_ff2[i].astype(BF16),
                       ln_g[i, 1][None], ln_b[i, 1][None], seq)
    return xf.reshape(batch, seq, d)
```

```python
import functools

import jax
import jax.numpy as jnp
from jax import lax
from jax.experimental import pallas as pl
from jax.experimental.pallas import tpu as pltpu

F32 = jnp.float32
BF16 = jnp.bfloat16

D_MODEL = 2048
DEPTH = 4
N_MIXERS = 3
DEEPNORM_ALPHA = float((2 * DEPTH) ** 0.25)
LN_EPS = 1e-5
RMS_EPS = 1e-6
D_FF = 4 * D_MODEL
N_MOD = 6
CONV_WIDTH = 4

RNN_WIDTH = (5 * D_MODEL) // 4
RNN_BLOCKS = 16
RNN_BLOCK_DIM = RNN_WIDTH // RNN_BLOCKS
RNN_GROUP = 640
RNN_GROUPS = RNN_WIDTH // RNN_GROUP
LRU_C = 8.0

GDN_HEAD_DIM = 128
GDN_QK_HEADS = D_MODEL // GDN_HEAD_DIM
GDN_V_HEADS = 2 * GDN_QK_HEADS
GDN_KEY_DIM = GDN_QK_HEADS * GDN_HEAD_DIM
GDN_VALUE_DIM = GDN_V_HEADS * GDN_HEAD_DIM
GDN_CONV_DIM = 2 * GDN_KEY_DIM + GDN_VALUE_DIM
GDN_PAIRS = GDN_QK_HEADS
GDN_GROUP = 8

GLA_HEADS = 4
GLA_KEY_DIM = D_MODEL // 2
GLA_VALUE_DIM = D_MODEL
GLA_HEAD_K = GLA_KEY_DIM // GLA_HEADS
GLA_HEAD_V = GLA_VALUE_DIM // GLA_HEADS
GLA_GATE_RANK = 16
GLA_GATE_NORMALIZER = 16.0

CHUNK = 64
LANES = 128
SUBLANES = 8
PROJ_TN = 1280
VMEM_LIMIT = 56 * 1024 * 1024


def _dot(a, b):
    return jnp.dot(a, b, preferred_element_type=F32)


def _dot_nt(a, b):
    return lax.dot_general(a, b, (((1,), (1,)), ((), ())), preferred_element_type=F32)


def _dot_tn(a, b):
    return lax.dot_general(a, b, (((0,), (0,)), ((), ())), preferred_element_type=F32)


def _bf(x):
    return x.astype(BF16)


def _split3(x):
    hi = _bf(x)
    r1 = x - hi.astype(F32)
    mid = _bf(r1)
    lo = _bf(r1 - mid.astype(F32))
    return hi, mid, lo


def _dot_mask_lhs(m, x):
    hi, mid, lo = _split3(x)
    return _dot(m, hi) + _dot(m, mid) + _dot(m, lo)


def _dot_mask_rhs(x, m):
    hi, mid, lo = _split3(x)
    return _dot(hi, m) + _dot(mid, m) + _dot(lo, m)


def _softplus(x):
    return jnp.maximum(x, 0.0) + jnp.log1p(jnp.exp(-jnp.abs(x)))


def _silu(x):
    return x * jax.nn.sigmoid(x)


def _params(*sem):
    return pltpu.CompilerParams(dimension_semantics=sem, vmem_limit_bytes=VMEM_LIMIT)


def _layer_norm_rows(r, g, b):
    mu = jnp.mean(r, axis=-1, keepdims=True)
    d = r - mu
    var = jnp.mean(d * d, axis=-1, keepdims=True)
    return d * lax.rsqrt(var + LN_EPS) * g + b


def _mod_kernel(c_ref, w_ref, b_ref, o_ref):
    c = c_ref[...]
    o_ref[0] = _dot(_bf(_silu(c)), _bf(w_ref[0])) + b_ref[0]


def _mod_call(c_pad, w_mod, b_mod):
    depth, d, n = w_mod.shape
    rows = c_pad.shape[0]
    tn = 1024
    return pl.pallas_call(
        _mod_kernel,
        grid=(depth, n // tn),
        in_specs=[
            pl.BlockSpec((rows, d), lambda l, j: (0, 0)),
            pl.BlockSpec((1, d, tn), lambda l, j: (l, 0, j)),
            pl.BlockSpec((1, 1, tn), lambda l, j: (l, 0, j)),
        ],
        out_specs=pl.BlockSpec((1, rows, tn), lambda l, j: (l, 0, j)),
        out_shape=jax.ShapeDtypeStruct((depth, rows, n), F32),
        compiler_params=_params("arbitrary", "arbitrary"),
        name="mod",
    )(c_pad, w_mod, b_mod.reshape(depth, 1, n))


def _proj_kernel(x_ref, sh_ref, sc_ref, w_ref, o_ref, h_scr):
    @pl.when(pl.program_id(1) == 0)
    def _():
        h_scr[...] = _bf(x_ref[...] * (1.0 + sc_ref[0]) + sh_ref[0])

    o_ref[...] = _dot(h_scr[...], w_ref[...])


def _proj_call(x, shift, scale, w, seq):
    n, d = x.shape
    nout = w.shape[1]
    tm = min(1024, seq)
    tn = PROJ_TN
    tpb = seq // tm
    return pl.pallas_call(
        _proj_kernel,
        grid=(n // tm, nout // tn),
        in_specs=[
            pl.BlockSpec((tm, d), lambda i, j: (i, 0)),
            pl.BlockSpec((1, 1, d), lambda i, j: (i // tpb, 0, 0)),
            pl.BlockSpec((1, 1, d), lambda i, j: (i // tpb, 0, 0)),
            pl.BlockSpec((d, tn), lambda i, j: (0, j)),
        ],
        out_specs=pl.BlockSpec((tm, tn), lambda i, j: (i, j)),
        out_shape=jax.ShapeDtypeStruct((n, nout), F32),
        scratch_shapes=[pltpu.VMEM((tm, d), BF16)],
        compiler_params=_params("arbitrary", "arbitrary"),
        name="proj",
    )(x, shift, scale, w)


def _outln_kernel(y_ref, w_ref, x_ref, gt_ref, g_ref, b_ref, o_ref):
    k = pl.program_id(1)

    @pl.when(k == 0)
    def _():
        o_ref[...] = jnp.zeros_like(o_ref)

    o_ref[...] += _dot(y_ref[...], w_ref[...])

    @pl.when(k == pl.num_programs(1) - 1)
    def _():
        r = DEEPNORM_ALPHA * x_ref[...] + (1.0 + gt_ref[0]) * o_ref[...]
        o_ref[...] = _layer_norm_rows(r, g_ref[...], b_ref[...])


def _outln_call(y, w, x, gate, ln_g, ln_b, seq):
    n, d = x.shape
    kdim = y.shape[1]
    tm = min(512, seq)
    tk = 512
    tpb = seq // tm
    return pl.pallas_call(
        _outln_kernel,
        grid=(n // tm, kdim // tk),
        in_specs=[
            pl.BlockSpec((tm, tk), lambda i, k: (i, k)),
            pl.BlockSpec((tk, d), lambda i, k: (k, 0)),
            pl.BlockSpec((tm, d), lambda i, k: (i, 0)),
            pl.BlockSpec((1, 1, d), lambda i, k: (i // tpb, 0, 0)),
            pl.BlockSpec((1, d), lambda i, k: (0, 0)),
            pl.BlockSpec((1, d), lambda i, k: (0, 0)),
        ],
        out_specs=pl.BlockSpec((tm, d), lambda i, k: (i, 0)),
        out_shape=jax.ShapeDtypeStruct((n, d), F32),
        compiler_params=_params("arbitrary", "arbitrary"),
        name="outln",
    )(y, w, x, gate, ln_g, ln_b)


def _mlp_kernel(x_ref, sh_ref, sc_ref, gt_ref, w1_ref, w2_ref, g_ref, b_ref, o_ref, h_scr):
    k = pl.program_id(1)

    @pl.when(k == 0)
    def _():
        h_scr[...] = _bf(x_ref[...] * (1.0 + sc_ref[0]) + sh_ref[0])
        o_ref[...] = jnp.zeros_like(o_ref)

    hid = jnp.maximum(_dot(h_scr[...], w1_ref[...]), 0.0)
    o_ref[...] += _dot(_bf(hid * hid), w2_ref[...])

    @pl.when(k == pl.num_programs(1) - 1)
    def _():
        r = DEEPNORM_ALPHA * x_ref[...] + (1.0 + gt_ref[0]) * o_ref[...]
        o_ref[...] = _layer_norm_rows(r, g_ref[...], b_ref[...])


def _mlp_call(x, shift, scale, gate, w1, w2, ln_g, ln_b, seq):
    n, d = x.shape
    dff = w1.shape[1]
    tm = min(512, seq)
    tf = 1024
    tpb = seq // tm
    vec = pl.BlockSpec((1, 1, d), lambda i, k: (i // tpb, 0, 0))
    return pl.pallas_call(
        _mlp_kernel,
        grid=(n // tm, dff // tf),
        in_specs=[
            pl.BlockSpec((tm, d), lambda i, k: (i, 0)),
            vec, vec, vec,
            pl.BlockSpec((d, tf), lambda i, k: (0, k)),
            pl.BlockSpec((tf, d), lambda i, k: (k, 0)),
            pl.BlockSpec((1, d), lambda i, k: (0, 0)),
            pl.BlockSpec((1, d), lambda i, k: (0, 0)),
        ],
        out_specs=pl.BlockSpec((tm, d), lambda i, k: (i, 0)),
        out_shape=jax.ShapeDtypeStruct((n, d), F32),
        scratch_shapes=[pltpu.VMEM((tm, d), BF16)],
        compiler_params=_params("arbitrary", "arbitrary"),
        name="mlp",
    )(x, shift, scale, gate, w1, w2, ln_g, ln_b)


def _rglru_kernel(gate_ref, rec_ref, cw_ref, cb_ref, wg_ref, bg_ref, lam_ref, o_ref,
                  xbuf, abuf, bbuf, hcar):
    tm = rec_ref.shape[0]

    @pl.when(pl.program_id(1) == 0)
    def _():
        xbuf[0:SUBLANES, :] = jnp.zeros((SUBLANES, RNN_WIDTH), F32)
        hcar[...] = jnp.zeros_like(hcar)

    xbuf[SUBLANES:SUBLANES + tm, :] = rec_ref[...]
    sp = _softplus(-lam_ref[...])
    for g in range(RNN_GROUPS):
        cs = slice(RNN_GROUP * g, RNN_GROUP * (g + 1))
        xr = cb_ref[:, cs]
        for j in range(CONV_WIDTH):
            r0 = SUBLANES - (CONV_WIDTH - 1) + j
            xr = xr + cw_ref[j:j + 1, cs] * xbuf[r0:r0 + tm, cs]
        pre = _dot(_bf(xr), wg_ref[g]) + bg_ref[g]
        r_gate = jax.nn.sigmoid(pre[:, :RNN_GROUP])
        i_gate = jax.nn.sigmoid(pre[:, RNN_GROUP:])
        log_a = (-LRU_C) * r_gate * sp[:, cs]
        a = jnp.exp(log_a)
        abuf[:, cs] = a
        bbuf[:, cs] = jnp.sqrt(-jnp.tanh(log_a) * (1.0 + a * a)) * (i_gate * xr)
    xbuf[0:SUBLANES, :] = xbuf[tm:tm + SUBLANES, :]

    row = lax.broadcasted_iota(jnp.int32, (SUBLANES, RNN_WIDTH), 0)

    def body(j, carry):
        r0 = pl.multiple_of(j * SUBLANES, SUBLANES)
        a = abuf[pl.ds(r0, SUBLANES), :]
        b = bbuf[pl.ds(r0, SUBLANES), :]
        for k in (1, 2, 4):
            a_s = jnp.where(row >= k, pltpu.roll(a, k, 0), 1.0)
            b_s = jnp.where(row >= k, pltpu.roll(b, k, 0), 0.0)
            b = a * b_s + b
            a = a * a_s
        h = a * carry + b
        bbuf[pl.ds(r0, SUBLANES), :] = h
        return h[SUBLANES - 1:SUBLANES, :]

    hcar[...] = lax.fori_loop(0, tm // SUBLANES, body, hcar[...])
    o_ref[...] = _bf(jax.nn.gelu(gate_ref[...], approximate=True) * bbuf[...])


def _rglru_core_call(u, conv_w, conv_b, wg, bg, lam, batch, seq):
    n = u.shape[0]
    tm = min(256, seq)
    tpb = seq // tm
    r = RNN_WIDTH
    const2 = lambda b, t: (0, 0)
    return pl.pallas_call(
        _rglru_kernel,
        grid=(batch, tpb),
        in_specs=[
            pl.BlockSpec((tm, r), lambda b, t: (b * tpb + t, 0)),
            pl.BlockSpec((tm, r), lambda b, t: (b * tpb + t, 1)),
            pl.BlockSpec((CONV_WIDTH, r), const2),
            pl.BlockSpec((1, r), const2),
            pl.BlockSpec((RNN_GROUPS, RNN_GROUP, 2 * RNN_GROUP), lambda b, t: (0, 0, 0)),
            pl.BlockSpec((RNN_GROUPS, 1, 2 * RNN_GROUP), lambda b, t: (0, 0, 0)),
            pl.BlockSpec((1, r), const2),
        ],
        out_specs=pl.BlockSpec((tm, r), lambda b, t: (b * tpb + t, 0)),
        out_shape=jax.ShapeDtypeStruct((n, r), BF16),
        scratch_shapes=[
            pltpu.VMEM((tm + SUBLANES, r), F32),
            pltpu.VMEM((tm, r), F32),
            pltpu.VMEM((tm, r), F32),
            pltpu.VMEM((1, r), F32),
        ],
        compiler_params=_params("arbitrary", "arbitrary"),
        name="rglru_core",
    )(u, u, conv_w, conv_b, wg, bg, lam)


def _gdn_kernel(qkv_ref, z_ref, br_ref, ar_ref, cw_ref, alog_ref, dtb_ref, nw_ref, o_ref,
                xbuf, qs, ks, vs, gcb, btb, gct, s_scr, o_scr):
    c = CHUNK
    hd = GDN_HEAD_DIM

    @pl.when(pl.program_id(1) == 0)
    def _():
        xbuf[0:SUBLANES, :] = jnp.zeros((SUBLANES, GDN_CONV_DIM), F32)
        s_scr[...] = jnp.zeros_like(s_scr)

    xbuf[SUBLANES:SUBLANES + c, :] = qkv_ref[...]

    beta = jax.nn.sigmoid(br_ref[...])
    g = -jnp.exp(alog_ref[...]) * _softplus(ar_ref[...] + dtb_ref[...])
    ti64 = lax.broadcasted_iota(jnp.int32, (c, c), 0)
    tj64 = lax.broadcasted_iota(jnp.int32, (c, c), 1)
    tri = jnp.where(ti64 >= tj64, 1.0, 0.0).astype(BF16)
    gc = _dot_mask_lhs(tri, g)
    g_t = jnp.concatenate([g, jnp.zeros_like(g)], axis=0).T
    ut = lax.broadcasted_iota(jnp.int32, (2 * c, 2 * c), 0)
    uj = lax.broadcasted_iota(jnp.int32, (2 * c, 2 * c), 1) & (c - 1)
    upper2 = jnp.where((ut <= uj) & (ut < c), 1.0, 0.0).astype(BF16)
    gct[...] = _dot_mask_rhs(g_t, upper2)
    for h in range(GDN_V_HEADS):
        gcb[h] = jnp.broadcast_to(gc[:, h:h + 1], (c, LANES))
        btb[h] = jnp.broadcast_to(beta[:, h:h + 1], (c, LANES))

    def conv_silu(col0, width):
        acc = None
        for j in range(CONV_WIDTH):
            r0 = SUBLANES - (CONV_WIDTH - 1) + j
            term = cw_ref[j:j + 1, col0:col0 + width] * xbuf[r0:r0 + c, col0:col0 + width]
            acc = term if acc is None else acc + term
        return _silu(acc)

    for p in range(GDN_PAIRS):
        q = conv_silu(hd * p, hd)
        q = q * lax.rsqrt(jnp.sum(q * q, axis=-1, keepdims=True) + RMS_EPS) * (hd ** -0.5)
        k = conv_silu(GDN_KEY_DIM + hd * p, hd)
        k = k * lax.rsqrt(jnp.sum(k * k, axis=-1, keepdims=True) + RMS_EPS)
        qs[p] = q
        ks[p] = k
        vs[p] = conv_silu(2 * GDN_KEY_DIM + 2 * hd * p, 2 * hd)
    xbuf[0:SUBLANES, :] = xbuf[c:c + SUBLANES, :]

    ti = lax.broadcasted_iota(jnp.int32, (c, LANES), 0)
    lane = lax.broadcasted_iota(jnp.int32, (c, LANES), 1)
    left = lane < c
    tj = lane & (c - 1)
    left_row = lax.broadcasted_iota(jnp.int32, (1, LANES), 1) < c
    eye = jnp.where(ti == tj, 1.0, 0.0)
    left2 = lax.broadcasted_iota(jnp.int32, (c, 2 * hd), 1) < hd
    left2s = lax.broadcasted_iota(jnp.int32, (hd, 2 * hd), 1) < hd
    zero_tile = jnp.zeros((c, hd), F32)

    def block_diag(x, msk):
        return jnp.concatenate([jnp.where(msk, x, 0.0), jnp.where(msk, 0.0, x)], axis=0)

    def gc_row(p):
        return jnp.where(left_row, gct[2 * p:2 * p + 1, :], gct[2 * p + 1:2 * p + 2, :])

    for g0 in range(0, GDN_PAIRS, GDN_GROUP):
        grp = range(g0, g0 + GDN_GROUP)
        kq, a_qk, w_pow, t_inv, wu, r1, r2, v_new = {}, {}, {}, {}, {}, {}, {}, {}
        for p in grp:
            k = ks[p]
            kq[p] = _dot_nt(_bf(jnp.concatenate([k, qs[p]], axis=0)),
                            _bf(jnp.concatenate([k, k], axis=0)))
        for p in grp:
            col = jnp.where(left, gcb[2 * p], gcb[2 * p + 1])
            decay = jnp.where(ti >= tj, jnp.exp(jnp.minimum(col - gc_row(p), 0.0)), 0.0)
            a_kk = (jnp.where(ti > tj, kq[p][:c] * decay, 0.0)
                    * jnp.where(left, btb[2 * p], btb[2 * p + 1]))
            a_qk[p] = kq[p][c:] * decay
            w_pow[p] = _dot(_bf(a_kk), _bf(block_diag(a_kk, left)))
            t_inv[p] = eye - a_kk
        for _ in range(4):
            for p in grp:
                r = _dot(_bf(jnp.concatenate([w_pow[p], t_inv[p]], axis=0)),
                         _bf(block_diag(w_pow[p], left)))
                w_pow[p] = r[:c]
                t_inv[p] = t_inv[p] + r[c:]
        for p in grp:
            t_inv[p] = t_inv[p] + _dot(_bf(t_inv[p]), _bf(block_diag(w_pow[p], left)))
        for p in grp:
            k = ks[p]
            bt0 = btb[2 * p]
            bt1 = btb[2 * p + 1]
            v_ss = vs[p]
            rhs = jnp.concatenate([
                jnp.concatenate([k * (bt0 * jnp.exp(gcb[2 * p])), zero_tile,
                                 v_ss[:, :hd] * bt0, zero_tile], axis=1),
                jnp.concatenate([zero_tile, k * (bt1 * jnp.exp(gcb[2 * p + 1])),
                                 zero_tile, v_ss[:, hd:] * bt1], axis=1)], axis=0)
            wu[p] = _dot(_bf(t_inv[p]), _bf(rhs))
        for p in grp:
            q = qs[p]
            q_dec = jnp.concatenate([q * jnp.exp(gcb[2 * p]), q * jnp.exp(gcb[2 * p + 1])], axis=1)
            r1[p] = _dot(_bf(jnp.concatenate([wu[p][:, :2 * hd], q_dec], axis=0)),
                         _bf(block_diag(s_scr[p], left2s)))
        for p in grp:
            v_new[p] = wu[p][:, 2 * hd:] - r1[p][:c]
            k = ks[p]
            gl = jnp.where(left_row, gcb[2 * p][c - 1:c, :], gcb[2 * p + 1][c - 1:c, :])
            k_tail = jnp.concatenate([k, k], axis=0).T * jnp.exp(gl - gc_row(p))
            r2[p] = _dot(_bf(jnp.concatenate([a_qk[p], k_tail], axis=0)),
                         _bf(block_diag(v_new[p], left2)))
        for p in grp:
            o_scr[p] = r1[p][c:] + r2[p][:c]
            cd = jnp.concatenate([jnp.exp(gcb[2 * p][c - 1:c, :]),
                                  jnp.exp(gcb[2 * p + 1][c - 1:c, :])], axis=1)
            s_scr[p] = s_scr[p] * cd + r2[p][c:]

    nw = nw_ref[...]
    for p in range(GDN_PAIRS):
        o_ss = o_scr[p]
        for s in range(2):
            cs = slice(2 * hd * p + hd * s, 2 * hd * p + hd * (s + 1))
            o = o_ss[:, hd * s:hd * (s + 1)]
            nrm = o * lax.rsqrt(jnp.mean(o * o, axis=-1, keepdims=True) + RMS_EPS)
            o_ref[:, cs] = _bf(nrm * nw * _silu(z_ref[:, cs]))


def _gdn_core_call(u, conv_w, a_log, dt_bias, norm_w, batch, seq):
    n = u.shape[0]
    c = CHUNK
    cpb = seq // c
    hd = GDN_HEAD_DIM
    row = lambda b, t: b * cpb + t
    const2 = lambda b, t: (0, 0)
    gate_blk = (GDN_CONV_DIM + GDN_VALUE_DIM) // LANES
    return pl.pallas_call(
        _gdn_kernel,
        grid=(batch, cpb),
        in_specs=[
            pl.BlockSpec((c, GDN_CONV_DIM), lambda b, t: (row(b, t), 0)),
            pl.BlockSpec((c, GDN_VALUE_DIM), lambda b, t: (row(b, t), GDN_CONV_DIM // GDN_VALUE_DIM)),
            pl.BlockSpec((c, LANES), lambda b, t: (row(b, t), gate_blk)),
            pl.BlockSpec((c, LANES), lambda b, t: (row(b, t), gate_blk + 1)),
            pl.BlockSpec((CONV_WIDTH, GDN_CONV_DIM), const2),
            pl.BlockSpec((1, LANES), const2),
            pl.BlockSpec((1, LANES), const2),
            pl.BlockSpec((1, hd), const2),
        ],
        out_specs=pl.BlockSpec((c, GDN_VALUE_DIM), lambda b, t: (row(b, t), 0)),
        out_shape=jax.ShapeDtypeStruct((n, GDN_VALUE_DIM), BF16),
        scratch_shapes=[
            pltpu.VMEM((c + SUBLANES, GDN_CONV_DIM), F32),
            pltpu.VMEM((GDN_PAIRS, c, hd), F32),
            pltpu.VMEM((GDN_PAIRS, c, hd), F32),
            pltpu.VMEM((GDN_PAIRS, c, 2 * hd), F32),
            pltpu.VMEM((GDN_V_HEADS, c, LANES), F32),
            pltpu.VMEM((GDN_V_HEADS, c, LANES), F32),
            pltpu.VMEM((2 * c, 2 * c), F32),
            pltpu.VMEM((GDN_PAIRS, hd, 2 * hd), F32),
            pltpu.VMEM((GDN_PAIRS, c, 2 * hd), F32),
        ],
        compiler_params=_params("arbitrary", "arbitrary"),
        name="gdn_core",
    )(u, u, u, u, conv_w, a_log, dt_bias, norm_w)


def _gla_kernel(q_ref, k_ref, v_ref, z_ref, al_ref, wup_ref, bal_ref, nw_ref, o_ref, s_scr):
    c = CHUNK
    dk = GLA_HEAD_K
    dv = GLA_HEAD_V

    @pl.when(pl.program_id(1) == 0)
    def _():
        s_scr[...] = jnp.zeros_like(s_scr)

    a_low = _bf(al_ref[...])
    ti = lax.broadcasted_iota(jnp.int32, (c, c), 0)
    tj = lax.broadcasted_iota(jnp.int32, (c, c), 1)
    causal = ti >= tj
    tri = jnp.where(causal, 1.0, 0.0).astype(BF16)
    nw = nw_ref[...]
    for h in range(GLA_HEADS):
        ks_ = slice(dk * h, dk * (h + 1))
        vs_ = slice(dv * h, dv * (h + 1))
        pre = _dot(a_low, wup_ref[:, ks_]) + bal_ref[:, ks_]
        log_alpha = (-1.0 / GLA_GATE_NORMALIZER) * _softplus(-pre)
        bcum = _dot_mask_lhs(tri, log_alpha)
        q = q_ref[:, ks_] * (dk ** -0.5)
        k = k_ref[:, ks_]
        v = _bf(v_ref[:, vs_])
        q_dec = _bf(q * jnp.exp(bcum))
        k_inv = _bf(k * jnp.exp(-bcum))
        a_qk = jnp.where(causal, _dot_nt(q_dec, k_inv), 0.0)
        s_t = s_scr[h]
        o = _dot_nt(q_dec, _bf(s_t)) + _dot(_bf(a_qk), v)
        b_last = bcum[c - 1:c, :]
        k_tail = _bf(k * jnp.exp(b_last - bcum))
        s_scr[h] = s_t * jnp.exp(b_last) + _dot_tn(v, k_tail)
        nrm = o * lax.rsqrt(jnp.mean(o * o, axis=-1, keepdims=True) + RMS_EPS)
        o_ref[:, vs_] = _bf(nrm * nw * _silu(z_ref[:, vs_]))


def _gla_core_call(u, w_up, b_alpha, norm_w, batch, seq):
    n = u.shape[0]
    c = CHUNK
    cpb = seq // c
    row = lambda b, t: b * cpb + t
    const2 = lambda b, t: (0, 0)
    return pl.pallas_call(
        _gla_kernel,
        grid=(batch, cpb),
        in_specs=[
            pl.BlockSpec((c, GLA_KEY_DIM), lambda b, t: (row(b, t), 0)),
            pl.BlockSpec((c, GLA_KEY_DIM), lambda b, t: (row(b, t), 1)),
            pl.BlockSpec((c, GLA_VALUE_DIM), lambda b, t: (row(b, t), 1)),
            pl.BlockSpec((c, GLA_VALUE_DIM), lambda b, t: (row(b, t), 2)),
            pl.BlockSpec((c, LANES), lambda b, t: (row(b, t), (2 * GLA_KEY_DIM + 2 * GLA_VALUE_DIM) // LANES)),
            pl.BlockSpec((LANES, GLA_KEY_DIM), const2),
            pl.BlockSpec((1, GLA_KEY_DIM), const2),
            pl.BlockSpec((1, GLA_HEAD_V), const2),
        ],
        out_specs=pl.BlockSpec((c, GLA_VALUE_DIM), lambda b, t: (row(b, t), 0)),
        out_shape=jax.ShapeDtypeStruct((n, GLA_VALUE_DIM), BF16),
        scratch_shapes=[pltpu.VMEM((GLA_HEADS, GLA_HEAD_V, GLA_HEAD_K), F32)],
        compiler_params=_params("arbitrary", "arbitrary"),
        name="gla_core",
    )(u, u, u, u, u, w_up, b_alpha, norm_w)


def _pad_cols(w, total):
    return jnp.pad(w, ((0, 0), (0, total - w.shape[1])))


def _rglru_gate_weights(w_rgate, b_rgate, w_igate, b_igate):
    per = RNN_GROUP // RNN_BLOCK_DIM
    eye = jnp.eye(per, dtype=F32)

    def dense(w):
        w = w.reshape(RNN_GROUPS, per, RNN_BLOCK_DIM, RNN_BLOCK_DIM)
        return jnp.einsum("gjde,jk->gjdke", w, eye).reshape(RNN_GROUPS, RNN_GROUP, RNN_GROUP)

    wg = jnp.concatenate([dense(w_rgate), dense(w_igate)], axis=-1).astype(BF16)
    bg = jnp.concatenate([b_rgate.reshape(RNN_GROUPS, 1, RNN_GROUP),
                          b_igate.reshape(RNN_GROUPS, 1, RNN_GROUP)], axis=-1)
    return wg, bg


def _gdn_in_weights(w_in):
    main = GDN_CONV_DIM + GDN_VALUE_DIM
    b_w = _pad_cols(w_in[:, main:main + GDN_V_HEADS], LANES)
    a_w = _pad_cols(w_in[:, main + GDN_V_HEADS:main + 2 * GDN_V_HEADS], LANES)
    w = jnp.concatenate([w_in[:, :main], b_w, a_w], axis=1)
    total = -(-w.shape[1] // PROJ_TN) * PROJ_TN
    return _pad_cols(w, total).astype(BF16)


def _gla_in_weights(w_in):
    total = -(-(w_in.shape[1] - GLA_GATE_RANK + LANES) // PROJ_TN) * PROJ_TN
    return _pad_cols(w_in, total).astype(BF16)


def kernel(x, c, ln_g, ln_b, w_mod, b_mod, w_ff1, w_ff2, rglru_w_in, rglru_conv_w, rglru_conv_b, rglru_w_rgate, rglru_b_rgate, rglru_w_igate, rglru_b_igate, rglru_lambda, rglru_w_out, gdn_w_in, gdn_conv_w, gdn_a_log, gdn_dt_bias, gdn_norm_w, gdn_w_out, gla_w_in, gla_w_alpha_up, gla_b_alpha, gla_norm_w, gla_w_out):
    batch, seq, d = x.shape
    depth = w_mod.shape[0]
    n = batch * seq
    xf = x.reshape(n, d)

    c_pad = jnp.pad(c, ((0, SUBLANES - batch % SUBLANES if batch % SUBLANES else 0), (0, 0)))
    mod = _mod_call(c_pad, w_mod, b_mod)[:, :batch]
    mod = mod.reshape(depth, batch, N_MOD, 1, d)

    for i in range(depth):
        sh_m, sc_m, gt_m, sh_f, sc_f, gt_f = (mod[i, :, j] for j in range(N_MOD))
        kind, slot = i % N_MIXERS, i // N_MIXERS
        if kind == 0:
            u = _proj_call(xf, sh_m, sc_m, rglru_w_in[slot].astype(BF16), seq)
            wg, bg = _rglru_gate_weights(rglru_w_rgate[slot], rglru_b_rgate[slot],
                                         rglru_w_igate[slot], rglru_b_igate[slot])
            y = _rglru_core_call(u, rglru_conv_w[slot], rglru_conv_b[slot][None], wg, bg,
                                 rglru_lambda[slot][None], batch, seq)
            w_out = rglru_w_out[slot]
        elif kind == 1:
            u = _proj_call(xf, sh_m, sc_m, _gdn_in_weights(gdn_w_in[slot]), seq)
            y = _gdn_core_call(u, gdn_conv_w[slot],
                               _pad_cols(gdn_a_log[slot][None], LANES),
                               _pad_cols(gdn_dt_bias[slot][None], LANES),
                               gdn_norm_w[slot][None], batch, seq)
            w_out = gdn_w_out[slot]
        else:
            u = _proj_call(xf, sh_m, sc_m, _gla_in_weights(gla_w_in[slot]), seq)
            w_up = jnp.pad(gla_w_alpha_up[slot], ((0, LANES - GLA_GATE_RANK), (0, 0))).astype(BF16)
            y = _gla_core_call(u, w_up, gla_b_alpha[slot][None], gla_norm_w[slot][None], batch, seq)
            w_out = gla_w_out[slot]
        xf = _outln_call(y, w_out.astype(BF16), xf, gt_m, ln_g[i, 0][None], ln_b[i, 0][None], seq)
        xf = _mlp_call(xf, sh_f, sc_f, gt_f, w_ff1[i].astype(BF16), w_ff2[i].astype(BF16),
                       ln_g[i, 1][None], ln_b[i, 1][None], seq)
    return xf.reshape(batch, seq, d)
```

```python
import functools

import jax
import jax.numpy as jnp
from jax import lax
from jax.experimental import pallas as pl
from jax.experimental.pallas import tpu as pltpu

F32 = jnp.float32
BF16 = jnp.bfloat16

D_MODEL = 2048
DEPTH = 4
N_MIXERS = 3
DEEPNORM_ALPHA = float((2 * DEPTH) ** 0.25)
LN_EPS = 1e-5
RMS_EPS = 1e-6
D_FF = 4 * D_MODEL
N_MOD = 6
MOD_ROWS = 8
CONV_WIDTH = 4

RNN_WIDTH = (5 * D_MODEL) // 4
RNN_BLOCKS = 16
RNN_BLOCK_DIM = RNN_WIDTH // RNN_BLOCKS
RNN_GROUP = 640
RNN_GROUPS = RNN_WIDTH // RNN_GROUP
LRU_C = 8.0

GDN_HEAD_DIM = 128
GDN_QK_HEADS = D_MODEL // GDN_HEAD_DIM
GDN_V_HEADS = 2 * GDN_QK_HEADS
GDN_KEY_DIM = GDN_QK_HEADS * GDN_HEAD_DIM
GDN_VALUE_DIM = GDN_V_HEADS * GDN_HEAD_DIM
GDN_CONV_DIM = 2 * GDN_KEY_DIM + GDN_VALUE_DIM
GDN_PAIRS = GDN_QK_HEADS
GDN_GROUP = 8

GLA_HEADS = 4
GLA_KEY_DIM = D_MODEL // 2
GLA_VALUE_DIM = D_MODEL
GLA_HEAD_K = GLA_KEY_DIM // GLA_HEADS
GLA_HEAD_V = GLA_VALUE_DIM // GLA_HEADS
GLA_GATE_RANK = 16
GLA_GATE_NORMALIZER = 16.0

CHUNK = 64
LANES = 128
SUBLANES = 8
PROJ_TN = 1280
VMEM_LIMIT = 56 * 1024 * 1024


def _dot(a, b):
    return jnp.dot(a, b, preferred_element_type=F32)


def _dot_nt(a, b):
    return lax.dot_general(a, b, (((1,), (1,)), ((), ())), preferred_element_type=F32)


def _dot_tn(a, b):
    return lax.dot_general(a, b, (((0,), (0,)), ((), ())), preferred_element_type=F32)


def _bf(x):
    return x.astype(BF16)


def _split3(x):
    hi = _bf(x)
    r1 = x - hi.astype(F32)
    mid = _bf(r1)
    lo = _bf(r1 - mid.astype(F32))
    return hi, mid, lo


def _dot_mask_lhs(m, x):
    hi, mid, lo = _split3(x)
    return _dot(m, hi) + _dot(m, mid) + _dot(m, lo)


def _dot_mask_rhs(x, m):
    hi, mid, lo = _split3(x)
    return _dot(hi, m) + _dot(mid, m) + _dot(lo, m)


def _softplus(x):
    return jnp.maximum(x, 0.0) + jnp.log1p(jnp.exp(-jnp.abs(x)))


def _silu(x):
    return x * jax.nn.sigmoid(x)


def _params(*sem):
    return pltpu.CompilerParams(dimension_semantics=sem, vmem_limit_bytes=VMEM_LIMIT)


def _layer_norm_rows(r, g, b):
    mu = jnp.mean(r, axis=-1, keepdims=True)
    d = r - mu
    var = jnp.mean(d * d, axis=-1, keepdims=True)
    return d * lax.rsqrt(var + LN_EPS) * g + b


def _mod_kernel(c_ref, w_ref, b_ref, o_ref):
    c = c_ref[...]
    o_ref[0] = _dot(_bf(_silu(c)), _bf(w_ref[0])) + b_ref[0]


def _mod_call(c_pad, w_mod, b_mod):
    depth, d, n = w_mod.shape
    rows = c_pad.shape[0]
    tn = 1024
    return pl.pallas_call(
        _mod_kernel,
        grid=(depth, n // tn),
        in_specs=[
            pl.BlockSpec((rows, d), lambda l, j: (0, 0)),
            pl.BlockSpec((1, d, tn), lambda l, j: (l, 0, j)),
            pl.BlockSpec((1, 1, tn), lambda l, j: (l, 0, j)),
        ],
        out_specs=pl.BlockSpec((1, rows, tn), lambda l, j: (l, 0, j)),
        out_shape=jax.ShapeDtypeStruct((depth, rows, n), F32),
        compiler_params=_params("arbitrary", "arbitrary"),
        name="mod",
    )(c_pad, w_mod, b_mod.reshape(depth, 1, n))


def _mod_spec(layer, which, tpb, d):
    return pl.BlockSpec((None, 1, d),
                        lambda i, *_: ((layer * MOD_ROWS + i // tpb) * N_MOD + which, 0, 0))


def _ln_spec(layer, which, d):
    return pl.BlockSpec((None, 1, d), lambda i, *_: (2 * layer + which, 0, 0))


def _proj_kernel(x_ref, sh_ref, sc_ref, w_ref, o_ref, h_scr):
    @pl.when(pl.program_id(1) == 0)
    def _():
        h_scr[...] = _bf(x_ref[...] * (1.0 + sc_ref[...]) + sh_ref[...])

    o_ref[...] = _dot(h_scr[...], w_ref[...])


def _proj_call(x, modv, layer, w, slot, seq):
    n, d = x.shape
    nout = w.shape[2]
    tm = min(1024, seq)
    tn = PROJ_TN
    tpb = seq // tm
    return pl.pallas_call(
        _proj_kernel,
        grid=(n // tm, nout // tn),
        in_specs=[
            pl.BlockSpec((tm, d), lambda i, j: (i, 0)),
            _mod_spec(layer, 0, tpb, d),
            _mod_spec(layer, 1, tpb, d),
            pl.BlockSpec((None, d, tn), lambda i, j: (slot, 0, j)),
        ],
        out_specs=pl.BlockSpec((tm, tn), lambda i, j: (i, j)),
        out_shape=jax.ShapeDtypeStruct((n, nout), F32),
        scratch_shapes=[pltpu.VMEM((tm, d), BF16)],
        compiler_params=_params("arbitrary", "arbitrary"),
        name="proj",
    )(x, modv, modv, w)


def _outln_kernel(y_ref, w_ref, x_ref, gt_ref, g_ref, b_ref, o_ref):
    r = DEEPNORM_ALPHA * x_ref[...] + (1.0 + gt_ref[...]) * _dot(y_ref[...], w_ref[...])
    o_ref[...] = _layer_norm_rows(r, g_ref[...], b_ref[...])


def _outln_call(y, w, slot, x, modv, lng, lnb, layer, seq):
    n, d = x.shape
    kdim = y.shape[1]
    tm = min(512, seq)
    tpb = seq // tm
    return pl.pallas_call(
        _outln_kernel,
        grid=(n // tm,),
        in_specs=[
            pl.BlockSpec((tm, kdim), lambda i: (i, 0)),
            pl.BlockSpec((None, kdim, d), lambda i: (slot, 0, 0), pipeline_mode=pl.Buffered(1)),
            pl.BlockSpec((tm, d), lambda i: (i, 0)),
            _mod_spec(layer, 2, tpb, d),
            _ln_spec(layer, 0, d),
            _ln_spec(layer, 0, d),
        ],
        out_specs=pl.BlockSpec((tm, d), lambda i: (i, 0)),
        out_shape=jax.ShapeDtypeStruct((n, d), F32),
        compiler_params=_params("arbitrary"),
        name="outln",
    )(y, w, x, modv, lng, lnb)


def _mlp_kernel(x_ref, sh_ref, sc_ref, gt_ref, w1_ref, w2_ref, g_ref, b_ref, o_ref, h_scr):
    k = pl.program_id(1)

    @pl.when(k == 0)
    def _():
        h_scr[...] = _bf(x_ref[...] * (1.0 + sc_ref[...]) + sh_ref[...])
        o_ref[...] = jnp.zeros_like(o_ref)

    hid = jnp.maximum(_dot(h_scr[...], w1_ref[...]), 0.0)
    o_ref[...] += _dot(_bf(hid * hid), w2_ref[...])

    @pl.when(k == pl.num_programs(1) - 1)
    def _():
        r = DEEPNORM_ALPHA * x_ref[...] + (1.0 + gt_ref[...]) * o_ref[...]
        o_ref[...] = _layer_norm_rows(r, g_ref[...], b_ref[...])


def _mlp_call(x, modv, w1, w2, lng, lnb, layer, seq):
    n, d = x.shape
    dff = w1.shape[2]
    tm = min(512, seq)
    tf = 1024
    tpb = seq // tm
    return pl.pallas_call(
        _mlp_kernel,
        grid=(n // tm, dff // tf),
        in_specs=[
            pl.BlockSpec((tm, d), lambda i, k: (i, 0)),
            _mod_spec(layer, 3, tpb, d),
            _mod_spec(layer, 4, tpb, d),
            _mod_spec(layer, 5, tpb, d),
            pl.BlockSpec((None, d, tf), lambda i, k: (layer, 0, k)),
            pl.BlockSpec((None, tf, d), lambda i, k: (layer, k, 0)),
            _ln_spec(layer, 1, d),
            _ln_spec(layer, 1, d),
        ],
        out_specs=pl.BlockSpec((tm, d), lambda i, k: (i, 0)),
        out_shape=jax.ShapeDtypeStruct((n, d), F32),
        scratch_shapes=[pltpu.VMEM((tm, d), BF16)],
        compiler_params=_params("arbitrary", "arbitrary"),
        name="mlp",
    )(x, modv, modv, modv, w1, w2, lng, lnb)


def _rglru_kernel(gate_ref, rec_ref, cw_ref, cb_ref, wg_ref, bg_ref, lam_ref, o_ref,
                  xbuf, abuf, bbuf, hcar):
    tm = rec_ref.shape[0]

    @pl.when(pl.program_id(1) == 0)
    def _():
        xbuf[0:SUBLANES, :] = jnp.zeros((SUBLANES, RNN_WIDTH), F32)
        hcar[...] = jnp.zeros_like(hcar)

    xbuf[SUBLANES:SUBLANES + tm, :] = rec_ref[...]
    sp = _softplus(-lam_ref[...])
    for g in range(RNN_GROUPS):
        cs = slice(RNN_GROUP * g, RNN_GROUP * (g + 1))
        xr = cb_ref[:, cs]
        for j in range(CONV_WIDTH):
            r0 = SUBLANES - (CONV_WIDTH - 1) + j
            xr = xr + cw_ref[j:j + 1, cs] * xbuf[r0:r0 + tm, cs]
        pre = _dot(_bf(xr), wg_ref[g]) + bg_ref[g]
        r_gate = jax.nn.sigmoid(pre[:, :RNN_GROUP])
        i_gate = jax.nn.sigmoid(pre[:, RNN_GROUP:])
        log_a = (-LRU_C) * r_gate * sp[:, cs]
        a = jnp.exp(log_a)
        abuf[:, cs] = a
        bbuf[:, cs] = jnp.sqrt(-jnp.tanh(log_a) * (1.0 + a * a)) * (i_gate * xr)
    xbuf[0:SUBLANES, :] = xbuf[tm:tm + SUBLANES, :]

    row = lax.broadcasted_iota(jnp.int32, (SUBLANES, RNN_WIDTH), 0)

    def body(j, carry):
        r0 = pl.multiple_of(j * SUBLANES, SUBLANES)
        a = abuf[pl.ds(r0, SUBLANES), :]
        b = bbuf[pl.ds(r0, SUBLANES), :]
        for k in (1, 2, 4):
            a_s = jnp.where(row >= k, pltpu.roll(a, k, 0), 1.0)
            b_s = jnp.where(row >= k, pltpu.roll(b, k, 0), 0.0)
            b = a * b_s + b
            a = a * a_s
        h = a * carry + b
        bbuf[pl.ds(r0, SUBLANES), :] = h
        return h[SUBLANES - 1:SUBLANES, :]

    hcar[...] = lax.fori_loop(0, tm // SUBLANES, body, hcar[...])
    o_ref[...] = _bf(jax.nn.gelu(gate_ref[...], approximate=True) * bbuf[...])


def _rglru_core_call(u, conv_w, conv_b, wg, bg, lam, batch, seq):
    n = u.shape[0]
    tm = min(256, seq)
    tpb = seq // tm
    r = RNN_WIDTH
    const2 = lambda b, t: (0, 0)
    return pl.pallas_call(
        _rglru_kernel,
        grid=(batch, tpb),
        in_specs=[
            pl.BlockSpec((tm, r), lambda b, t: (b * tpb + t, 0)),
            pl.BlockSpec((tm, r), lambda b, t: (b * tpb + t, 1)),
            pl.BlockSpec((CONV_WIDTH, r), const2),
            pl.BlockSpec((1, r), const2),
            pl.BlockSpec((RNN_GROUPS, RNN_GROUP, 2 * RNN_GROUP), lambda b, t: (0, 0, 0)),
            pl.BlockSpec((RNN_GROUPS, 1, 2 * RNN_GROUP), lambda b, t: (0, 0, 0)),
            pl.BlockSpec((1, r), const2),
        ],
        out_specs=pl.BlockSpec((tm, r), lambda b, t: (b * tpb + t, 0)),
        out_shape=jax.ShapeDtypeStruct((n, r), BF16),
        scratch_shapes=[
            pltpu.VMEM((tm + SUBLANES, r), F32),
            pltpu.VMEM((tm, r), F32),
            pltpu.VMEM((tm, r), F32),
            pltpu.VMEM((1, r), F32),
        ],
        compiler_params=_params("arbitrary", "arbitrary"),
        name="rglru_core",
    )(u, u, conv_w, conv_b, wg, bg, lam)


def _gdn_kernel(qkv_ref, z_ref, br_ref, ar_ref, cw_ref, alog_ref, dtb_ref, nw_ref, o_ref,
                xbuf, qs, ks, vs, gcb, btb, gct, s_scr, o_scr):
    c = CHUNK
    hd = GDN_HEAD_DIM

    @pl.when(pl.program_id(1) == 0)
    def _():
        xbuf[0:SUBLANES, :] = jnp.zeros((SUBLANES, GDN_CONV_DIM), F32)
        s_scr[...] = jnp.zeros_like(s_scr)

    xbuf[SUBLANES:SUBLANES + c, :] = qkv_ref[...]

    beta = jax.nn.sigmoid(br_ref[...])
    g = -jnp.exp(alog_ref[...]) * _softplus(ar_ref[...] + dtb_ref[...])
    ti64 = lax.broadcasted_iota(jnp.int32, (c, c), 0)
    tj64 = lax.broadcasted_iota(jnp.int32, (c, c), 1)
    tri = jnp.where(ti64 >= tj64, 1.0, 0.0).astype(BF16)
    gc = _dot_mask_lhs(tri, g)
    g_t = jnp.concatenate([g, jnp.zeros_like(g)], axis=0).T
    ut = lax.broadcasted_iota(jnp.int32, (2 * c, 2 * c), 0)
    uj = lax.broadcasted_iota(jnp.int32, (2 * c, 2 * c), 1) & (c - 1)
    upper2 = jnp.where((ut <= uj) & (ut < c), 1.0, 0.0).astype(BF16)
    gct[...] = _dot_mask_rhs(g_t, upper2)
    for h in range(GDN_V_HEADS):
        gcb[h] = jnp.broadcast_to(gc[:, h:h + 1], (c, LANES))
        btb[h] = jnp.broadcast_to(beta[:, h:h + 1], (c, LANES))

    def conv_silu(col0, width):
        acc = None
        for j in range(CONV_WIDTH):
            r0 = SUBLANES - (CONV_WIDTH - 1) + j
            term = cw_ref[j:j + 1, col0:col0 + width] * xbuf[r0:r0 + c, col0:col0 + width]
            acc = term if acc is None else acc + term
        return _silu(acc)

    for p in range(GDN_PAIRS):
        q = conv_silu(hd * p, hd)
        q = q * lax.rsqrt(jnp.sum(q * q, axis=-1, keepdims=True) + RMS_EPS) * (hd ** -0.5)
        k = conv_silu(GDN_KEY_DIM + hd * p, hd)
        k = k * lax.rsqrt(jnp.sum(k * k, axis=-1, keepdims=True) + RMS_EPS)
        qs[p] = q
        ks[p] = k
        vs[p] = conv_silu(2 * GDN_KEY_DIM + 2 * hd * p, 2 * hd)
    xbuf[0:SUBLANES, :] = xbuf[c:c + SUBLANES, :]

    ti = lax.broadcasted_iota(jnp.int32, (c, LANES), 0)
    lane = lax.broadcasted_iota(jnp.int32, (c, LANES), 1)
    left = lane < c
    tj = lane & (c - 1)
    left_row = lax.broadcasted_iota(jnp.int32, (1, LANES), 1) < c
    eye = jnp.where(ti == tj, 1.0, 0.0)
    left2 = lax.broadcasted_iota(jnp.int32, (c, 2 * hd), 1) < hd
    left2s = lax.broadcasted_iota(jnp.int32, (hd, 2 * hd), 1) < hd
    zero_tile = jnp.zeros((c, hd), F32)

    def block_diag(x, msk):
        return jnp.concatenate([jnp.where(msk, x, 0.0), jnp.where(msk, 0.0, x)], axis=0)

    def gc_row(p):
        return jnp.where(left_row, gct[2 * p:2 * p + 1, :], gct[2 * p + 1:2 * p + 2, :])

    for g0 in range(0, GDN_PAIRS, GDN_GROUP):
        grp = range(g0, g0 + GDN_GROUP)
        kq, a_qk, w_pow, t_inv, wu, r1, r2, v_new = {}, {}, {}, {}, {}, {}, {}, {}
        for p in grp:
            k = ks[p]
            kq[p] = _dot_nt(_bf(jnp.concatenate([k, qs[p]], axis=0)),
                            _bf(jnp.concatenate([k, k], axis=0)))
        for p in grp:
            col = jnp.where(left, gcb[2 * p], gcb[2 * p + 1])
            decay = jnp.where(ti >= tj, jnp.exp(jnp.minimum(col - gc_row(p), 0.0)), 0.0)
            a_kk = (jnp.where(ti > tj, kq[p][:c] * decay, 0.0)
                    * jnp.where(left, btb[2 * p], btb[2 * p + 1]))
            a_qk[p] = kq[p][c:] * decay
            w_pow[p] = _dot(_bf(a_kk), _bf(block_diag(a_kk, left)))
            t_inv[p] = eye - a_kk
        for _ in range(4):
            for p in grp:
                r = _dot(_bf(jnp.concatenate([w_pow[p], t_inv[p]], axis=0)),
                         _bf(block_diag(w_pow[p], left)))
                w_pow[p] = r[:c]
                t_inv[p] = t_inv[p] + r[c:]
        for p in grp:
            t_inv[p] = t_inv[p] + _dot(_bf(t_inv[p]), _bf(block_diag(w_pow[p], left)))
        for p in grp:
            k = ks[p]
            bt0 = btb[2 * p]
            bt1 = btb[2 * p + 1]
            v_ss = vs[p]
            rhs = jnp.concatenate([
                jnp.concatenate([k * (bt0 * jnp.exp(gcb[2 * p])), zero_tile,
                                 v_ss[:, :hd] * bt0, zero_tile], axis=1),
                jnp.concatenate([zero_tile, k * (bt1 * jnp.exp(gcb[2 * p + 1])),
                                 zero_tile, v_ss[:, hd:] * bt1], axis=1)], axis=0)
            wu[p] = _dot(_bf(t_inv[p]), _bf(rhs))
        for p in grp:
            q = qs[p]
            q_dec = jnp.concatenate([q * jnp.exp(gcb[2 * p]), q * jnp.exp(gcb[2 * p + 1])], axis=1)
            r1[p] = _dot(_bf(jnp.concatenate([wu[p][:, :2 * hd], q_dec], axis=0)),
                         _bf(block_diag(s_scr[p], left2s)))
        for p in grp:
            v_new[p] = wu[p][:, 2 * hd:] - r1[p][:c]
            k = ks[p]
            gl = jnp.where(left_row, gcb[2 * p][c - 1:c, :], gcb[2 * p + 1][c - 1:c, :])
            k_tail = jnp.concatenate([k, k], axis=0).T * jnp.exp(gl - gc_row(p))
            r2[p] = _dot(_bf(jnp.concatenate([a_qk[p], k_tail], axis=0)),
                         _bf(block_diag(v_new[p], left2)))
        for p in grp:
            o_scr[p] = r1[p][c:] + r2[p][:c]
            cd = jnp.concatenate([jnp.exp(gcb[2 * p][c - 1:c, :]),
                                  jnp.exp(gcb[2 * p + 1][c - 1:c, :])], axis=1)
            s_scr[p] = s_scr[p] * cd + r2[p][c:]

    nw = nw_ref[...]
    for p in range(GDN_PAIRS):
        o_ss = o_scr[p]
        for s in range(2):
            cs = slice(2 * hd * p + hd * s, 2 * hd * p + hd * (s + 1))
            o = o_ss[:, hd * s:hd * (s + 1)]
            nrm = o * lax.rsqrt(jnp.mean(o * o, axis=-1, keepdims=True) + RMS_EPS)
            o_ref[:, cs] = _bf(nrm * nw * _silu(z_ref[:, cs]))


def _gdn_core_call(u, conv_w, a_log, dt_bias, norm_w, batch, seq):
    n = u.shape[0]
    c = CHUNK
    cpb = seq // c
    hd = GDN_HEAD_DIM
    row = lambda b, t: b * cpb + t
    const2 = lambda b, t: (0, 0)
    gate_blk = (GDN_CONV_DIM + GDN_VALUE_DIM) // LANES
    return pl.pallas_call(
        _gdn_kernel,
        grid=(batch, cpb),
        in_specs=[
            pl.BlockSpec((c, GDN_CONV_DIM), lambda b, t: (row(b, t), 0)),
            pl.BlockSpec((c, GDN_VALUE_DIM), lambda b, t: (row(b, t), GDN_CONV_DIM // GDN_VALUE_DIM)),
            pl.BlockSpec((c, LANES), lambda b, t: (row(b, t), gate_blk)),
            pl.BlockSpec((c, LANES), lambda b, t: (row(b, t), gate_blk + 1)),
            pl.BlockSpec((CONV_WIDTH, GDN_CONV_DIM), const2),
            pl.BlockSpec((1, LANES), const2),
            pl.BlockSpec((1, LANES), const2),
            pl.BlockSpec((1, hd), const2),
        ],
        out_specs=pl.BlockSpec((c, GDN_VALUE_DIM), lambda b, t: (row(b, t), 0)),
        out_shape=jax.ShapeDtypeStruct((n, GDN_VALUE_DIM), BF16),
        scratch_shapes=[
            pltpu.VMEM((c + SUBLANES, GDN_CONV_DIM), F32),
            pltpu.VMEM((GDN_PAIRS, c, hd), F32),
            pltpu.VMEM((GDN_PAIRS, c, hd), F32),
            pltpu.VMEM((GDN_PAIRS, c, 2 * hd), F32),
            pltpu.VMEM((GDN_V_HEADS, c, LANES), F32),
            pltpu.VMEM((GDN_V_HEADS, c, LANES), F32),
            pltpu.VMEM((2 * c, 2 * c), F32),
            pltpu.VMEM((GDN_PAIRS, hd, 2 * hd), F32),
            pltpu.VMEM((GDN_PAIRS, c, 2 * hd), F32),
        ],
        compiler_params=_params("arbitrary", "arbitrary"),
        name="gdn_core",
    )(u, u, u, u, conv_w, a_log, dt_bias, norm_w)


def _gla_kernel(q_ref, k_ref, v_ref, z_ref, al_ref, wup_ref, bal_ref, nw_ref, o_ref, s_scr, bc_scr):
    c = CHUNK
    dk = GLA_HEAD_K
    dv = GLA_HEAD_V

    nb = q_ref.shape[0]

    @pl.when(pl.program_id(0) == 0)
    def _():
        s_scr[...] = jnp.zeros_like(s_scr)

    ti = lax.broadcasted_iota(jnp.int32, (c, c), 0)
    tj = lax.broadcasted_iota(jnp.int32, (c, c), 1)
    causal = ti >= tj
    tri = jnp.where(causal, 1.0, 0.0).astype(BF16)
    nw = nw_ref[...]
    chains = [(b, h) for b in range(nb) for h in range(GLA_HEADS)]
    pre = {}
    for b in range(nb):
        pre[b] = _dot(_bf(al_ref[b]), wup_ref[...]) + bal_ref[...]
    for b in range(nb):
        log_alpha = (-1.0 / GLA_GATE_NORMALIZER) * _softplus(-pre[b])
        bc_scr[b] = _dot_mask_lhs(tri, log_alpha)
    q_dec, a_qk, o = {}, {}, {}
    for b, h in chains:
        ks_ = slice(dk * h, dk * (h + 1))
        bcum = bc_scr[b, :, ks_]
        q_dec[b, h] = _bf(q_ref[b, :, ks_] * (dk ** -0.5) * jnp.exp(bcum))
        a_qk[b, h] = _dot_nt(q_dec[b, h], _bf(k_ref[b, :, ks_] * jnp.exp(-bcum)))
    for b, h in chains:
        v = _bf(v_ref[b, :, dv * h:dv * (h + 1)])
        s_t = s_scr[b * GLA_HEADS + h]
        o[b, h] = (_dot_nt(q_dec[b, h], _bf(s_t))
                   + _dot(_bf(jnp.where(causal, a_qk[b, h], 0.0)), v))
    for b, h in chains:
        ks_ = slice(dk * h, dk * (h + 1))
        bcum = bc_scr[b, :, ks_]
        b_last = bcum[c - 1:c, :]
        k_tail = _bf(k_ref[b, :, ks_] * jnp.exp(b_last - bcum))
        i = b * GLA_HEADS + h
        s_scr[i] = s_scr[i] * jnp.exp(b_last) + _dot_tn(_bf(v_ref[b, :, dv * h:dv * (h + 1)]), k_tail)
    for b, h in chains:
        vs_ = slice(dv * h, dv * (h + 1))
        oh = o[b, h]
        nrm = oh * lax.rsqrt(jnp.mean(oh * oh, axis=-1, keepdims=True) + RMS_EPS)
        o_ref[b, :, vs_] = _bf(nrm * nw * _silu(z_ref[b, :, vs_]))


def _gla_core_call(u, w_up, b_alpha, norm_w, batch, seq):
    c = CHUNK
    u3 = u.reshape(batch, seq, u.shape[1])
    const2 = lambda t: (0, 0)
    out = pl.pallas_call(
        _gla_kernel,
        grid=(seq // c,),
        in_specs=[
            pl.BlockSpec((batch, c, GLA_KEY_DIM), lambda t: (0, t, 0)),
            pl.BlockSpec((batch, c, GLA_KEY_DIM), lambda t: (0, t, 1)),
            pl.BlockSpec((batch, c, GLA_VALUE_DIM), lambda t: (0, t, 1)),
            pl.BlockSpec((batch, c, GLA_VALUE_DIM), lambda t: (0, t, 2)),
            pl.BlockSpec((batch, c, LANES), lambda t: (0, t, (2 * GLA_KEY_DIM + 2 * GLA_VALUE_DIM) // LANES)),
            pl.BlockSpec((LANES, GLA_KEY_DIM), const2),
            pl.BlockSpec((1, GLA_KEY_DIM), const2),
            pl.BlockSpec((1, GLA_HEAD_V), const2),
        ],
        out_specs=pl.BlockSpec((batch, c, GLA_VALUE_DIM), lambda t: (0, t, 0)),
        out_shape=jax.ShapeDtypeStruct((batch, seq, GLA_VALUE_DIM), BF16),
        scratch_shapes=[pltpu.VMEM((batch * GLA_HEADS, GLA_HEAD_V, GLA_HEAD_K), F32),
                        pltpu.VMEM((batch, c, GLA_KEY_DIM), F32)],
        compiler_params=_params("arbitrary"),
        name="gla_core",
    )(u3, u3, u3, u3, u3, w_up, b_alpha, norm_w)
    return out.reshape(batch * seq, GLA_VALUE_DIM)


def _pad_cols(w, total):
    return jnp.pad(w, ((0, 0), (0, total - w.shape[1])))


def _rglru_gate_weights(w_rgate, b_rgate, w_igate, b_igate):
    per = RNN_GROUP // RNN_BLOCK_DIM
    eye = jnp.eye(per, dtype=F32)

    def dense(w):
        w = w.reshape(RNN_GROUPS, per, RNN_BLOCK_DIM, RNN_BLOCK_DIM)
        return jnp.einsum("gjde,jk->gjdke", w, eye).reshape(RNN_GROUPS, RNN_GROUP, RNN_GROUP)

    wg = jnp.concatenate([dense(w_rgate), dense(w_igate)], axis=-1).astype(BF16)
    bg = jnp.concatenate([b_rgate.reshape(RNN_GROUPS, 1, RNN_GROUP),
                          b_igate.reshape(RNN_GROUPS, 1, RNN_GROUP)], axis=-1)
    return wg, bg


def _gdn_in_weights(w_in):
    main = GDN_CONV_DIM + GDN_VALUE_DIM
    b_w = _pad_cols(w_in[:, main:main + GDN_V_HEADS], LANES)
    a_w = _pad_cols(w_in[:, main + GDN_V_HEADS:main + 2 * GDN_V_HEADS], LANES)
    w = jnp.concatenate([w_in[:, :main], b_w, a_w], axis=1)
    total = -(-w.shape[1] // PROJ_TN) * PROJ_TN
    return _pad_cols(w, total).astype(BF16)


def _gla_in_weights(w_in):
    total = -(-(w_in.shape[1] - GLA_GATE_RANK + LANES) // PROJ_TN) * PROJ_TN
    return _pad_cols(w_in, total).astype(BF16)


def kernel(x, c, ln_g, ln_b, w_mod, b_mod, w_ff1, w_ff2, rglru_w_in, rglru_conv_w, rglru_conv_b, rglru_w_rgate, rglru_b_rgate, rglru_w_igate, rglru_b_igate, rglru_lambda, rglru_w_out, gdn_w_in, gdn_conv_w, gdn_a_log, gdn_dt_bias, gdn_norm_w, gdn_w_out, gla_w_in, gla_w_alpha_up, gla_b_alpha, gla_norm_w, gla_w_out):
    batch, seq, d = x.shape
    depth = w_mod.shape[0]
    n = batch * seq
    xf = x.reshape(n, d)

    assert batch <= MOD_ROWS
    c_pad = jnp.pad(c, ((0, MOD_ROWS - batch), (0, 0)))
    modv = _mod_call(c_pad, w_mod, b_mod).reshape(depth * MOD_ROWS * N_MOD, 1, d)
    lng = ln_g.reshape(depth * 2, 1, d)
    lnb = ln_b.reshape(depth * 2, 1, d)
    w_ff1_b = w_ff1.astype(BF16)
    w_ff2_b = w_ff2.astype(BF16)
    rglru_w_in_b = rglru_w_in.astype(BF16)
    rglru_w_out_b = rglru_w_out.astype(BF16)
    gdn_w_out_b = gdn_w_out.astype(BF16)
    gla_w_out_b = gla_w_out.astype(BF16)

    for i in range(depth):
        kind, slot = i % N_MIXERS, i // N_MIXERS
        if kind == 0:
            u = _proj_call(xf, modv, i, rglru_w_in_b, slot, seq)
            wg, bg = _rglru_gate_weights(rglru_w_rgate[slot], rglru_b_rgate[slot],
                                         rglru_w_igate[slot], rglru_b_igate[slot])
            y = _rglru_core_call(u, rglru_conv_w[slot], rglru_conv_b[slot][None], wg, bg,
                                 rglru_lambda[slot][None], batch, seq)
            w_out = rglru_w_out_b
        elif kind == 1:
            u = _proj_call(xf, modv, i, _gdn_in_weights(gdn_w_in[slot])[None], 0, seq)
            y = _gdn_core_call(u, gdn_conv_w[slot],
                               _pad_cols(gdn_a_log[slot][None], LANES),
                               _pad_cols(gdn_dt_bias[slot][None], LANES),
                               gdn_norm_w[slot][None], batch, seq)
            w_out = gdn_w_out_b
        else:
            u = _proj_call(xf, modv, i, _gla_in_weights(gla_w_in[slot])[None], 0, seq)
            w_up = jnp.pad(gla_w_alpha_up[slot], ((0, LANES - GLA_GATE_RANK), (0, 0))).astype(BF16)
            y = _gla_core_call(u, w_up, gla_b_alpha[slot][None], gla_norm_w[slot][None], batch, seq)
            w_out = gla_w_out_b
        xf = _outln_call(y, w_out, slot, xf, modv, lng, lnb, i, seq)
        xf = _mlp_call(xf, modv, w_ff1_b, w_ff2_b, lng, lnb, i, seq)
    return xf.reshape(batch, seq, d)
```

```python
import functools

import jax
import jax.numpy as jnp
from jax import lax
from jax.experimental import pallas as pl
from jax.experimental.pallas import tpu as pltpu

F32 = jnp.float32
BF16 = jnp.bfloat16

D_MODEL = 2048
DEPTH = 4
N_MIXERS = 3
DEEPNORM_ALPHA = float((2 * DEPTH) ** 0.25)
LN_EPS = 1e-5
RMS_EPS = 1e-6
D_FF = 4 * D_MODEL
N_MOD = 6
MOD_ROWS = 8
CONV_WIDTH = 4

RNN_WIDTH = (5 * D_MODEL) // 4
RNN_BLOCKS = 16
RNN_BLOCK_DIM = RNN_WIDTH // RNN_BLOCKS
RNN_GROUP = 640
RNN_GROUPS = RNN_WIDTH // RNN_GROUP
LRU_C = 8.0

GDN_HEAD_DIM = 128
GDN_QK_HEADS = D_MODEL // GDN_HEAD_DIM
GDN_V_HEADS = 2 * GDN_QK_HEADS
GDN_KEY_DIM = GDN_QK_HEADS * GDN_HEAD_DIM
GDN_VALUE_DIM = GDN_V_HEADS * GDN_HEAD_DIM
GDN_CONV_DIM = 2 * GDN_KEY_DIM + GDN_VALUE_DIM
GDN_PAIRS = GDN_QK_HEADS
GDN_GROUP = 8

GLA_HEADS = 4
GLA_KEY_DIM = D_MODEL // 2
GLA_VALUE_DIM = D_MODEL
GLA_HEAD_K = GLA_KEY_DIM // GLA_HEADS
GLA_HEAD_V = GLA_VALUE_DIM // GLA_HEADS
GLA_GATE_RANK = 16
GLA_GATE_NORMALIZER = 16.0

CHUNK = 64
LANES = 128
SUBLANES = 8
PROJ_TN = 1024
VMEM_LIMIT = 56 * 1024 * 1024


def _dot(a, b):
    return jnp.dot(a, b, preferred_element_type=F32)


def _dot_nt(a, b):
    return lax.dot_general(a, b, (((1,), (1,)), ((), ())), preferred_element_type=F32)


def _dot_tn(a, b):
    return lax.dot_general(a, b, (((0,), (0,)), ((), ())), preferred_element_type=F32)


def _bf(x):
    return x.astype(BF16)


def _split3(x):
    hi = _bf(x)
    r1 = x - hi.astype(F32)
    mid = _bf(r1)
    lo = _bf(r1 - mid.astype(F32))
    return hi, mid, lo


def _dot_mask_lhs(m, x):
    hi, mid, lo = _split3(x)
    return _dot(m, hi) + _dot(m, mid) + _dot(m, lo)


def _dot_mask_rhs(x, m):
    hi, mid, lo = _split3(x)
    return _dot(hi, m) + _dot(mid, m) + _dot(lo, m)


def _softplus(x):
    return jnp.maximum(x, 0.0) + jnp.log1p(jnp.exp(-jnp.abs(x)))


def _silu(x):
    return x * jax.nn.sigmoid(x)


def _params(*sem):
    return pltpu.CompilerParams(dimension_semantics=sem, vmem_limit_bytes=VMEM_LIMIT)


def _layer_norm_rows(r, g, b):
    mu = jnp.mean(r, axis=-1, keepdims=True)
    d = r - mu
    var = jnp.mean(d * d, axis=-1, keepdims=True)
    return d * lax.rsqrt(var + LN_EPS) * g + b


def _mod_kernel(c_ref, w_ref, b_ref, o_ref):
    c = c_ref[...]
    o_ref[0] = _dot(_bf(_silu(c)), _bf(w_ref[0])) + b_ref[0]


def _mod_call(c_pad, w_mod, b_mod):
    depth, d, n = w_mod.shape
    rows = c_pad.shape[0]
    tn = 1024
    return pl.pallas_call(
        _mod_kernel,
        grid=(depth, n // tn),
        in_specs=[
            pl.BlockSpec((rows, d), lambda l, j: (0, 0)),
            pl.BlockSpec((1, d, tn), lambda l, j: (l, 0, j)),
            pl.BlockSpec((1, 1, tn), lambda l, j: (l, 0, j)),
        ],
        out_specs=pl.BlockSpec((1, rows, tn), lambda l, j: (l, 0, j)),
        out_shape=jax.ShapeDtypeStruct((depth, rows, n), F32),
        compiler_params=_params("arbitrary", "arbitrary"),
        name="mod",
    )(c_pad, w_mod, b_mod.reshape(depth, 1, n))


def _mod_spec(layer, which, tpb, d):
    return pl.BlockSpec((None, 1, d),
                        lambda i, *_: ((layer * MOD_ROWS + i // tpb) * N_MOD + which, 0, 0))


def _ln_spec(layer, which, d):
    return pl.BlockSpec((None, 1, d), lambda i, *_: (2 * layer + which, 0, 0))


def _proj_kernel(has_small, x_ref, sh_ref, sc_ref, w_ref, *rest):
    if has_small:
        ws_ref, o_ref, os_ref, h_scr = rest
    else:
        o_ref, h_scr = rest

    @pl.when(pl.program_id(1) == 0)
    def _():
        h = _bf(x_ref[...] * (1.0 + sc_ref[...]) + sh_ref[...])
        h_scr[...] = h
        if has_small:
            os_ref[...] = _dot(h, ws_ref[...])

    o_ref[...] = _dot(h_scr[...], _bf(w_ref[...]))


def _proj_call(x, modv, layer, w, slot, nout, seq, w_small=None):
    n, d = x.shape
    tm = min(1024, seq)
    tn = PROJ_TN
    tpb = seq // tm
    has_small = w_small is not None
    in_specs = [
        pl.BlockSpec((tm, d), lambda i, j: (i, 0)),
        _mod_spec(layer, 0, tpb, d),
        _mod_spec(layer, 1, tpb, d),
        pl.BlockSpec((None, d, tn), lambda i, j: (slot, 0, j)),
    ]
    out_specs = [pl.BlockSpec((tm, tn), lambda i, j: (i, j))]
    out_shape = [jax.ShapeDtypeStruct((n, nout), F32)]
    args = [x, modv, modv, w]
    if has_small:
        ns = w_small.shape[1]
        in_specs.append(pl.BlockSpec((d, ns), lambda i, j: (0, 0)))
        out_specs.append(pl.BlockSpec((tm, ns), lambda i, j: (i, 0)))
        out_shape.append(jax.ShapeDtypeStruct((n, ns), F32))
        args.append(w_small)
    outs = pl.pallas_call(
        functools.partial(_proj_kernel, has_small),
        grid=(n // tm, nout // tn),
        in_specs=in_specs,
        out_specs=out_specs,
        out_shape=out_shape,
        scratch_shapes=[pltpu.VMEM((tm, d), BF16)],
        compiler_params=_params("arbitrary", "arbitrary"),
        name="proj",
    )(*args)
    return outs if has_small else (outs[0], None)


def _outln_kernel(y_ref, w_ref, x_ref, gt_ref, g_ref, b_ref, o_ref):
    r = DEEPNORM_ALPHA * x_ref[...] + (1.0 + gt_ref[...]) * _dot(y_ref[...], w_ref[...])
    o_ref[...] = _layer_norm_rows(r, g_ref[...], b_ref[...])


def _outln_call(y, w, slot, x, modv, lng, lnb, layer, seq):
    n, d = x.shape
    kdim = y.shape[1]
    tm = min(512, seq)
    tpb = seq // tm
    return pl.pallas_call(
        _outln_kernel,
        grid=(n // tm,),
        in_specs=[
            pl.BlockSpec((tm, kdim), lambda i: (i, 0)),
            pl.BlockSpec((None, kdim, d), lambda i: (slot, 0, 0), pipeline_mode=pl.Buffered(1)),
            pl.BlockSpec((tm, d), lambda i: (i, 0)),
            _mod_spec(layer, 2, tpb, d),
            _ln_spec(layer, 0, d),
            _ln_spec(layer, 0, d),
        ],
        out_specs=pl.BlockSpec((tm, d), lambda i: (i, 0)),
        out_shape=jax.ShapeDtypeStruct((n, d), F32),
        compiler_params=_params("arbitrary"),
        name="outln",
    )(y, w, x, modv, lng, lnb)


def _mlp_kernel(x_ref, sh_ref, sc_ref, gt_ref, w1_ref, w2_ref, g_ref, b_ref, o_ref, h_scr):
    k = pl.program_id(1)

    @pl.when(k == 0)
    def _():
        h_scr[...] = _bf(x_ref[...] * (1.0 + sc_ref[...]) + sh_ref[...])
        o_ref[...] = jnp.zeros_like(o_ref)

    hid = jnp.maximum(_dot(h_scr[...], _bf(w1_ref[...])), 0.0)
    o_ref[...] += _dot(_bf(hid * hid), _bf(w2_ref[...]))

    @pl.when(k == pl.num_programs(1) - 1)
    def _():
        r = DEEPNORM_ALPHA * x_ref[...] + (1.0 + gt_ref[...]) * o_ref[...]
        o_ref[...] = _layer_norm_rows(r, g_ref[...], b_ref[...])


def _mlp_call(x, modv, w1, w2, lng, lnb, layer, seq):
    n, d = x.shape
    dff = w1.shape[2]
    tm = min(1024, seq)
    tf = 512
    tpb = seq // tm
    return pl.pallas_call(
        _mlp_kernel,
        grid=(n // tm, dff // tf),
        in_specs=[
            pl.BlockSpec((tm, d), lambda i, k: (i, 0), pipeline_mode=pl.Buffered(1)),
            _mod_spec(layer, 3, tpb, d),
            _mod_spec(layer, 4, tpb, d),
            _mod_spec(layer, 5, tpb, d),
            pl.BlockSpec((None, d, tf), lambda i, k: (layer, 0, k)),
            pl.BlockSpec((None, tf, d), lambda i, k: (layer, k, 0)),
            _ln_spec(layer, 1, d),
            _ln_spec(layer, 1, d),
        ],
        out_specs=pl.BlockSpec((tm, d), lambda i, k: (i, 0)),
        out_shape=jax.ShapeDtypeStruct((n, d), F32),
        scratch_shapes=[pltpu.VMEM((tm, d), BF16)],
        compiler_params=_params("arbitrary", "arbitrary"),
        name="mlp",
    )(x, modv, modv, modv, w1, w2, lng, lnb)


def _rglru_kernel(gate_ref, rec_ref, cw_ref, cb_ref, wg_ref, bg_ref, lam_ref, o_ref,
                  xbuf, abuf, bbuf, hcar):
    tm = rec_ref.shape[0]

    @pl.when(pl.program_id(1) == 0)
    def _():
        xbuf[0:SUBLANES, :] = jnp.zeros((SUBLANES, RNN_WIDTH), F32)
        hcar[...] = jnp.zeros_like(hcar)

    xbuf[SUBLANES:SUBLANES + tm, :] = rec_ref[...]
    sp = _softplus(-lam_ref[...])
    for g in range(RNN_GROUPS):
        cs = slice(RNN_GROUP * g, RNN_GROUP * (g + 1))
        xr = cb_ref[:, cs]
        for j in range(CONV_WIDTH):
            r0 = SUBLANES - (CONV_WIDTH - 1) + j
            xr = xr + cw_ref[j:j + 1, cs] * xbuf[r0:r0 + tm, cs]
        pre = _dot(_bf(xr), wg_ref[g]) + bg_ref[g]
        r_gate = jax.nn.sigmoid(pre[:, :RNN_GROUP])
        i_gate = jax.nn.sigmoid(pre[:, RNN_GROUP:])
        log_a = (-LRU_C) * r_gate * sp[:, cs]
        a = jnp.exp(log_a)
        abuf[:, cs] = a
        bbuf[:, cs] = jnp.sqrt(-jnp.tanh(log_a) * (1.0 + a * a)) * (i_gate * xr)
    xbuf[0:SUBLANES, :] = xbuf[tm:tm + SUBLANES, :]

    row = lax.broadcasted_iota(jnp.int32, (SUBLANES, RNN_WIDTH), 0)

    def body(j, carry):
        r0 = pl.multiple_of(j * SUBLANES, SUBLANES)
        a = abuf[pl.ds(r0, SUBLANES), :]
        b = bbuf[pl.ds(r0, SUBLANES), :]
        for k in (1, 2, 4):
            a_s = jnp.where(row >= k, pltpu.roll(a, k, 0), 1.0)
            b_s = jnp.where(row >= k, pltpu.roll(b, k, 0), 0.0)
            b = a * b_s + b
            a = a * a_s
        h = a * carry + b
        bbuf[pl.ds(r0, SUBLANES), :] = h
        return h[SUBLANES - 1:SUBLANES, :]

    hcar[...] = lax.fori_loop(0, tm // SUBLANES, body, hcar[...])
    o_ref[...] = _bf(jax.nn.gelu(gate_ref[...], approximate=True) * bbuf[...])


def _rglru_core_call(u, conv_w, conv_b, wg, bg, lam, batch, seq):
    n = u.shape[0]
    tm = min(256, seq)
    tpb = seq // tm
    r = RNN_WIDTH
    const2 = lambda b, t: (0, 0)
    return pl.pallas_call(
        _rglru_kernel,
        grid=(batch, tpb),
        in_specs=[
            pl.BlockSpec((tm, r), lambda b, t: (b * tpb + t, 0)),
            pl.BlockSpec((tm, r), lambda b, t: (b * tpb + t, 1)),
            pl.BlockSpec((CONV_WIDTH, r), const2),
            pl.BlockSpec((1, r), const2),
            pl.BlockSpec((RNN_GROUPS, RNN_GROUP, 2 * RNN_GROUP), lambda b, t: (0, 0, 0)),
            pl.BlockSpec((RNN_GROUPS, 1, 2 * RNN_GROUP), lambda b, t: (0, 0, 0)),
            pl.BlockSpec((1, r), const2),
        ],
        out_specs=pl.BlockSpec((tm, r), lambda b, t: (b * tpb + t, 0)),
        out_shape=jax.ShapeDtypeStruct((n, r), BF16),
        scratch_shapes=[
            pltpu.VMEM((tm + SUBLANES, r), F32),
            pltpu.VMEM((tm, r), F32),
            pltpu.VMEM((tm, r), F32),
            pltpu.VMEM((1, r), F32),
        ],
        compiler_params=_params("arbitrary", "arbitrary"),
        name="rglru_core",
    )(u, u, conv_w, conv_b, wg, bg, lam)


def _gdn_kernel(qkv_ref, z_ref, br_ref, ar_ref, cw_ref, alog_ref, dtb_ref, nw_ref, o_ref,
                xbuf, qs, ks, vs, gcb, btb, gct, s_scr, o_scr):
    c = CHUNK
    hd = GDN_HEAD_DIM

    @pl.when(pl.program_id(1) == 0)
    def _():
        xbuf[0:SUBLANES, :] = jnp.zeros((SUBLANES, GDN_CONV_DIM), F32)
        s_scr[...] = jnp.zeros_like(s_scr)

    xbuf[SUBLANES:SUBLANES + c, :] = qkv_ref[...]

    beta = jax.nn.sigmoid(br_ref[...])
    g = -jnp.exp(alog_ref[...]) * _softplus(ar_ref[...] + dtb_ref[...])
    ti64 = lax.broadcasted_iota(jnp.int32, (c, c), 0)
    tj64 = lax.broadcasted_iota(jnp.int32, (c, c), 1)
    tri = jnp.where(ti64 >= tj64, 1.0, 0.0).astype(BF16)
    gc = _dot_mask_lhs(tri, g)
    g_t = jnp.concatenate([g, jnp.zeros_like(g)], axis=0).T
    ut = lax.broadcasted_iota(jnp.int32, (2 * c, 2 * c), 0)
    uj = lax.broadcasted_iota(jnp.int32, (2 * c, 2 * c), 1) & (c - 1)
    upper2 = jnp.where((ut <= uj) & (ut < c), 1.0, 0.0).astype(BF16)
    gct[...] = _dot_mask_rhs(g_t, upper2)
    for h in range(GDN_V_HEADS):
        gcb[h] = jnp.broadcast_to(gc[:, h:h + 1], (c, LANES))
        btb[h] = jnp.broadcast_to(beta[:, h:h + 1], (c, LANES))

    def conv_silu(col0, width):
        acc = None
        for j in range(CONV_WIDTH):
            r0 = SUBLANES - (CONV_WIDTH - 1) + j
            term = cw_ref[j:j + 1, col0:col0 + width] * xbuf[r0:r0 + c, col0:col0 + width]
            acc = term if acc is None else acc + term
        return _silu(acc)

    for p in range(GDN_PAIRS):
        q = conv_silu(hd * p, hd)
        q = q * lax.rsqrt(jnp.sum(q * q, axis=-1, keepdims=True) + RMS_EPS) * (hd ** -0.5)
        k = conv_silu(GDN_KEY_DIM + hd * p, hd)
        k = k * lax.rsqrt(jnp.sum(k * k, axis=-1, keepdims=True) + RMS_EPS)
        qs[p] = q
        ks[p] = k
        vs[p] = conv_silu(2 * GDN_KEY_DIM + 2 * hd * p, 2 * hd)
    xbuf[0:SUBLANES, :] = xbuf[c:c + SUBLANES, :]

    ti = lax.broadcasted_iota(jnp.int32, (c, LANES), 0)
    lane = lax.broadcasted_iota(jnp.int32, (c, LANES), 1)
    left = lane < c
    tj = lane & (c - 1)
    left_row = lax.broadcasted_iota(jnp.int32, (1, LANES), 1) < c
    eye = jnp.where(ti == tj, 1.0, 0.0)
    left2 = lax.broadcasted_iota(jnp.int32, (c, 2 * hd), 1) < hd
    left2s = lax.broadcasted_iota(jnp.int32, (hd, 2 * hd), 1) < hd
    zero_tile = jnp.zeros((c, hd), F32)

    def block_diag(x, msk):
        return jnp.concatenate([jnp.where(msk, x, 0.0), jnp.where(msk, 0.0, x)], axis=0)

    def gc_row(p):
        return jnp.where(left_row, gct[2 * p:2 * p + 1, :], gct[2 * p + 1:2 * p + 2, :])

    for g0 in range(0, GDN_PAIRS, GDN_GROUP):
        grp = range(g0, g0 + GDN_GROUP)
        kq, a_qk, w_pow, t_inv, wu, r1, r2, v_new = {}, {}, {}, {}, {}, {}, {}, {}
        for p in grp:
            k = ks[p]
            kq[p] = _dot_nt(_bf(jnp.concatenate([k, qs[p]], axis=0)),
                            _bf(jnp.concatenate([k, k], axis=0)))
        for p in grp:
            col = jnp.where(left, gcb[2 * p], gcb[2 * p + 1])
            decay = jnp.where(ti >= tj, jnp.exp(jnp.minimum(col - gc_row(p), 0.0)), 0.0)
            a_kk = (jnp.where(ti > tj, kq[p][:c] * decay, 0.0)
                    * jnp.where(left, btb[2 * p], btb[2 * p + 1]))
            a_qk[p] = kq[p][c:] * decay
            w_pow[p] = _dot(_bf(a_kk), _bf(block_diag(a_kk, left)))
            t_inv[p] = eye - a_kk
        for _ in range(4):
            for p in grp:
                r = _dot(_bf(jnp.concatenate([w_pow[p], t_inv[p]], axis=0)),
                         _bf(block_diag(w_pow[p], left)))
                w_pow[p] = r[:c]
                t_inv[p] = t_inv[p] + r[c:]
        for p in grp:
            t_inv[p] = t_inv[p] + _dot(_bf(t_inv[p]), _bf(block_diag(w_pow[p], left)))
        for p in grp:
            k = ks[p]
            bt0 = btb[2 * p]
            bt1 = btb[2 * p + 1]
            v_ss = vs[p]
            rhs = jnp.concatenate([
                jnp.concatenate([k * (bt0 * jnp.exp(gcb[2 * p])), zero_tile,
                                 v_ss[:, :hd] * bt0, zero_tile], axis=1),
                jnp.concatenate([zero_tile, k * (bt1 * jnp.exp(gcb[2 * p + 1])),
                                 zero_tile, v_ss[:, hd:] * bt1], axis=1)], axis=0)
            wu[p] = _dot(_bf(t_inv[p]), _bf(rhs))
        for p in grp:
            q = qs[p]
            q_dec = jnp.concatenate([q * jnp.exp(gcb[2 * p]), q * jnp.exp(gcb[2 * p + 1])], axis=1)
            r1[p] = _dot(_bf(jnp.concatenate([wu[p][:, :2 * hd], q_dec], axis=0)),
                         _bf(block_diag(s_scr[p], left2s)))
        for p in grp:
            v_new[p] = wu[p][:, 2 * hd:] - r1[p][:c]
            k = ks[p]
            gl = jnp.where(left_row, gcb[2 * p][c - 1:c, :], gcb[2 * p + 1][c - 1:c, :])
            k_tail = jnp.concatenate([k, k], axis=0).T * jnp.exp(gl - gc_row(p))
            r2[p] = _dot(_bf(jnp.concatenate([a_qk[p], k_tail], axis=0)),
                         _bf(block_diag(v_new[p], left2)))
        for p in grp:
            o_scr[p] = r1[p][c:] + r2[p][:c]
            cd = jnp.concatenate([jnp.exp(gcb[2 * p][c - 1:c, :]),
                                  jnp.exp(gcb[2 * p + 1][c - 1:c, :])], axis=1)
            s_scr[p] = s_scr[p] * cd + r2[p][c:]

    nw = nw_ref[...]
    for p in range(GDN_PAIRS):
        o_ss = o_scr[p]
        for s in range(2):
            cs = slice(2 * hd * p + hd * s, 2 * hd * p + hd * (s + 1))
            o = o_ss[:, hd * s:hd * (s + 1)]
            nrm = o * lax.rsqrt(jnp.mean(o * o, axis=-1, keepdims=True) + RMS_EPS)
            o_ref[:, cs] = _bf(nrm * nw * _silu(z_ref[:, cs]))


def _gdn_core_call(u, u_gate, conv_w, a_log, dt_bias, norm_w, batch, seq):
    n = u.shape[0]
    c = CHUNK
    cpb = seq // c
    hd = GDN_HEAD_DIM
    row = lambda b, t: b * cpb + t
    const2 = lambda b, t: (0, 0)
    return pl.pallas_call(
        _gdn_kernel,
        grid=(batch, cpb),
        in_specs=[
            pl.BlockSpec((c, GDN_CONV_DIM), lambda b, t: (row(b, t), 0)),
            pl.BlockSpec((c, GDN_VALUE_DIM), lambda b, t: (row(b, t), GDN_CONV_DIM // GDN_VALUE_DIM)),
            pl.BlockSpec((c, LANES), lambda b, t: (row(b, t), 0)),
            pl.BlockSpec((c, LANES), lambda b, t: (row(b, t), 1)),
            pl.BlockSpec((CONV_WIDTH, GDN_CONV_DIM), const2),
            pl.BlockSpec((1, LANES), const2),
            pl.BlockSpec((1, LANES), const2),
            pl.BlockSpec((1, hd), const2),
        ],
        out_specs=pl.BlockSpec((c, GDN_VALUE_DIM), lambda b, t: (row(b, t), 0)),
        out_shape=jax.ShapeDtypeStruct((n, GDN_VALUE_DIM), BF16),
        scratch_shapes=[
            pltpu.VMEM((c + SUBLANES, GDN_CONV_DIM), F32),
            pltpu.VMEM((GDN_PAIRS, c, hd), F32),
            pltpu.VMEM((GDN_PAIRS, c, hd), F32),
            pltpu.VMEM((GDN_PAIRS, c, 2 * hd), F32),
            pltpu.VMEM((GDN_V_HEADS, c, LANES), F32),
            pltpu.VMEM((GDN_V_HEADS, c, LANES), F32),
            pltpu.VMEM((2 * c, 2 * c), F32),
            pltpu.VMEM((GDN_PAIRS, hd, 2 * hd), F32),
            pltpu.VMEM((GDN_PAIRS, c, 2 * hd), F32),
        ],
        compiler_params=_params("arbitrary", "arbitrary"),
        name="gdn_core",
    )(u, u, u_gate, u_gate, conv_w, a_log, dt_bias, norm_w)


def _gla_kernel(q_ref, k_ref, v_ref, z_ref, al_ref, wup_ref, bal_ref, nw_ref, o_ref, s_scr, bc_scr):
    c = CHUNK
    dk = GLA_HEAD_K
    dv = GLA_HEAD_V

    nb = q_ref.shape[0]

    @pl.when(pl.program_id(0) == 0)
    def _():
        s_scr[...] = jnp.zeros_like(s_scr)

    ti = lax.broadcasted_iota(jnp.int32, (c, c), 0)
    tj = lax.broadcasted_iota(jnp.int32, (c, c), 1)
    causal = ti >= tj
    tri = jnp.where(causal, 1.0, 0.0).astype(BF16)
    nw = nw_ref[...]
    chains = [(b, h) for b in range(nb) for h in range(GLA_HEADS)]
    pre = {}
    for b in range(nb):
        pre[b] = _dot(_bf(al_ref[b]), wup_ref[...]) + bal_ref[...]
    for b in range(nb):
        log_alpha = (-1.0 / GLA_GATE_NORMALIZER) * _softplus(-pre[b])
        bc_scr[b] = _dot_mask_lhs(tri, log_alpha)
    q_dec, a_qk, o = {}, {}, {}
    for b, h in chains:
        ks_ = slice(dk * h, dk * (h + 1))
        bcum = bc_scr[b, :, ks_]
        q_dec[b, h] = _bf(q_ref[b, :, ks_] * (dk ** -0.5) * jnp.exp(bcum))
        a_qk[b, h] = _dot_nt(q_dec[b, h], _bf(k_ref[b, :, ks_] * jnp.exp(-bcum)))
    for b, h in chains:
        v = _bf(v_ref[b, :, dv * h:dv * (h + 1)])
        s_t = s_scr[b * GLA_HEADS + h]
        o[b, h] = (_dot_nt(q_dec[b, h], _bf(s_t))
                   + _dot(_bf(jnp.where(causal, a_qk[b, h], 0.0)), v))
    for b, h in chains:
        ks_ = slice(dk * h, dk * (h + 1))
        bcum = bc_scr[b, :, ks_]
        b_last = bcum[c - 1:c, :]
        k_tail = _bf(k_ref[b, :, ks_] * jnp.exp(b_last - bcum))
        i = b * GLA_HEADS + h
        s_scr[i] = s_scr[i] * jnp.exp(b_last) + _dot_tn(_bf(v_ref[b, :, dv * h:dv * (h + 1)]), k_tail)
    for b, h in chains:
        vs_ = slice(dv * h, dv * (h + 1))
        oh = o[b, h]
        nrm = oh * lax.rsqrt(jnp.mean(oh * oh, axis=-1, keepdims=True) + RMS_EPS)
        o_ref[b, :, vs_] = _bf(nrm * nw * _silu(z_ref[b, :, vs_]))


def _gla_core_call(u, u_gate, w_up, b_alpha, norm_w, batch, seq):
    c = CHUNK
    u3 = u.reshape(batch, seq, u.shape[1])
    g3 = u_gate.reshape(batch, seq, u_gate.shape[1])
    const2 = lambda t: (0, 0)
    out = pl.pallas_call(
        _gla_kernel,
        grid=(seq // c,),
        in_specs=[
            pl.BlockSpec((batch, c, GLA_KEY_DIM), lambda t: (0, t, 0)),
            pl.BlockSpec((batch, c, GLA_KEY_DIM), lambda t: (0, t, 1)),
            pl.BlockSpec((batch, c, GLA_VALUE_DIM), lambda t: (0, t, 1)),
            pl.BlockSpec((batch, c, GLA_VALUE_DIM), lambda t: (0, t, 2)),
            pl.BlockSpec((batch, c, LANES), lambda t: (0, t, 0)),
            pl.BlockSpec((LANES, GLA_KEY_DIM), const2),
            pl.BlockSpec((1, GLA_KEY_DIM), const2),
            pl.BlockSpec((1, GLA_HEAD_V), const2),
        ],
        out_specs=pl.BlockSpec((batch, c, GLA_VALUE_DIM), lambda t: (0, t, 0)),
        out_shape=jax.ShapeDtypeStruct((batch, seq, GLA_VALUE_DIM), BF16),
        scratch_shapes=[pltpu.VMEM((batch * GLA_HEADS, GLA_HEAD_V, GLA_HEAD_K), F32),
                        pltpu.VMEM((batch, c, GLA_KEY_DIM), F32)],
        compiler_params=_params("arbitrary"),
        name="gla_core",
    )(u3, u3, u3, u3, g3, w_up, b_alpha, norm_w)
    return out.reshape(batch * seq, GLA_VALUE_DIM)


def _pad_cols(w, total):
    return jnp.pad(w, ((0, 0), (0, total - w.shape[1])))


def _rglru_gate_weights(w_rgate, b_rgate, w_igate, b_igate):
    per = RNN_GROUP // RNN_BLOCK_DIM
    eye = jnp.eye(per, dtype=F32)

    def dense(w):
        w = w.reshape(RNN_GROUPS, per, RNN_BLOCK_DIM, RNN_BLOCK_DIM)
        return jnp.einsum("gjde,jk->gjdke", w, eye).reshape(RNN_GROUPS, RNN_GROUP, RNN_GROUP)

    wg = jnp.concatenate([dense(w_rgate), dense(w_igate)], axis=-1).astype(BF16)
    bg = jnp.concatenate([b_rgate.reshape(RNN_GROUPS, 1, RNN_GROUP),
                          b_igate.reshape(RNN_GROUPS, 1, RNN_GROUP)], axis=-1)
    return wg, bg


def _gdn_gate_weights(w_in):
    main = GDN_CONV_DIM + GDN_VALUE_DIM
    b_w = _pad_cols(w_in[:, main:main + GDN_V_HEADS], LANES)
    a_w = _pad_cols(w_in[:, main + GDN_V_HEADS:main + 2 * GDN_V_HEADS], LANES)
    return jnp.concatenate([b_w, a_w], axis=1).astype(BF16)


def _gla_gate_weights(w_in):
    main = 2 * GLA_KEY_DIM + 2 * GLA_VALUE_DIM
    return _pad_cols(w_in[:, main:main + GLA_GATE_RANK], LANES).astype(BF16)


def kernel(x, c, ln_g, ln_b, w_mod, b_mod, w_ff1, w_ff2, rglru_w_in, rglru_conv_w, rglru_conv_b, rglru_w_rgate, rglru_b_rgate, rglru_w_igate, rglru_b_igate, rglru_lambda, rglru_w_out, gdn_w_in, gdn_conv_w, gdn_a_log, gdn_dt_bias, gdn_norm_w, gdn_w_out, gla_w_in, gla_w_alpha_up, gla_b_alpha, gla_norm_w, gla_w_out):
    batch, seq, d = x.shape
    depth = w_mod.shape[0]
    n = batch * seq
    xf = x.reshape(n, d)

    assert batch <= MOD_ROWS
    c_pad = jnp.pad(c, ((0, MOD_ROWS - batch), (0, 0)))
    modv = _mod_call(c_pad, w_mod, b_mod).reshape(depth * MOD_ROWS * N_MOD, 1, d)
    lng = ln_g.reshape(depth * 2, 1, d)
    lnb = ln_b.reshape(depth * 2, 1, d)
    rglru_w_out_b = rglru_w_out.astype(BF16)
    gdn_w_out_b = gdn_w_out.astype(BF16)
    gla_w_out_b = gla_w_out.astype(BF16)

    for i in range(depth):
        kind, slot = i % N_MIXERS, i // N_MIXERS
        if kind == 0:
            u, _ = _proj_call(xf, modv, i, rglru_w_in, slot, 2 * RNN_WIDTH, seq)
            wg, bg = _rglru_gate_weights(rglru_w_rgate[slot], rglru_b_rgate[slot],
                                         rglru_w_igate[slot], rglru_b_igate[slot])
            y = _rglru_core_call(u, rglru_conv_w[slot], rglru_conv_b[slot][None], wg, bg,
                                 rglru_lambda[slot][None], batch, seq)
            w_out = rglru_w_out_b
        elif kind == 1:
            u, u_gate = _proj_call(xf, modv, i, gdn_w_in, slot, GDN_CONV_DIM + GDN_VALUE_DIM, seq,
                                   w_small=_gdn_gate_weights(gdn_w_in[slot]))
            y = _gdn_core_call(u, u_gate, gdn_conv_w[slot],
                               _pad_cols(gdn_a_log[slot][None], LANES),
                               _pad_cols(gdn_dt_bias[slot][None], LANES),
                               gdn_norm_w[slot][None], batch, seq)
            w_out = gdn_w_out_b
        else:
            u, u_gate = _proj_call(xf, modv, i, gla_w_in, slot, 2 * GLA_KEY_DIM + 2 * GLA_VALUE_DIM, seq,
                                   w_small=_gla_gate_weights(gla_w_in[slot]))
            w_up = jnp.pad(gla_w_alpha_up[slot], ((0, LANES - GLA_GATE_RANK), (0, 0))).astype(BF16)
            y = _gla_core_call(u, u_gate, w_up, gla_b_alpha[slot][None], gla_norm_w[slot][None], batch, seq)
            w_out = gla_w_out_b
        xf = _outln_call(y, w_out, slot, xf, modv, lng, lnb, i, seq)
        xf = _mlp_call(xf, modv, w_ff1, w_ff2, lng, lnb, i, seq)
    return xf.reshape(batch, seq, d)
```

```python
import functools

import jax
import jax.numpy as jnp
from jax import lax
from jax.experimental import pallas as pl
from jax.experimental.pallas import tpu as pltpu

F32 = jnp.float32
BF16 = jnp.bfloat16

D_MODEL = 2048
DEPTH = 4
N_MIXERS = 3
DEEPNORM_ALPHA = float((2 * DEPTH) ** 0.25)
LN_EPS = 1e-5
RMS_EPS = 1e-6
D_FF = 4 * D_MODEL
N_MOD = 6
MOD_ROWS = 8
CONV_WIDTH = 4

RNN_WIDTH = (5 * D_MODEL) // 4
RNN_BLOCKS = 16
RNN_BLOCK_DIM = RNN_WIDTH // RNN_BLOCKS
RNN_GROUP = 640
RNN_GROUPS = RNN_WIDTH // RNN_GROUP
LRU_C = 8.0

GDN_HEAD_DIM = 128
GDN_QK_HEADS = D_MODEL // GDN_HEAD_DIM
GDN_V_HEADS = 2 * GDN_QK_HEADS
GDN_KEY_DIM = GDN_QK_HEADS * GDN_HEAD_DIM
GDN_VALUE_DIM = GDN_V_HEADS * GDN_HEAD_DIM
GDN_CONV_DIM = 2 * GDN_KEY_DIM + GDN_VALUE_DIM
GDN_PAIRS = GDN_QK_HEADS
GDN_GROUP = 8

GLA_HEADS = 4
GLA_KEY_DIM = D_MODEL // 2
GLA_VALUE_DIM = D_MODEL
GLA_HEAD_K = GLA_KEY_DIM // GLA_HEADS
GLA_HEAD_V = GLA_VALUE_DIM // GLA_HEADS
GLA_GATE_RANK = 16
GLA_GATE_NORMALIZER = 16.0

CHUNK = 64
LANES = 128
SUBLANES = 8
VMEM_LIMIT = 56 * 1024 * 1024


def _dot(a, b):
    return jnp.dot(a, b, preferred_element_type=F32)


def _dot_nt(a, b):
    return lax.dot_general(a, b, (((1,), (1,)), ((), ())), preferred_element_type=F32)


def _dot_tn(a, b):
    return lax.dot_general(a, b, (((0,), (0,)), ((), ())), preferred_element_type=F32)


def _bf(x):
    return x.astype(BF16)


def _split3(x):
    hi = _bf(x)
    r1 = x - hi.astype(F32)
    mid = _bf(r1)
    lo = _bf(r1 - mid.astype(F32))
    return hi, mid, lo


def _dot_mask_lhs(m, x):
    hi, mid, lo = _split3(x)
    return _dot(m, hi) + _dot(m, mid) + _dot(m, lo)


def _dot_mask_rhs(x, m):
    hi, mid, lo = _split3(x)
    return _dot(hi, m) + _dot(mid, m) + _dot(lo, m)


def _softplus(x):
    return jnp.maximum(x, 0.0) + jnp.log1p(jnp.exp(-jnp.abs(x)))


def _silu(x):
    return x * jax.nn.sigmoid(x)


def _params(*sem):
    return pltpu.CompilerParams(dimension_semantics=sem, vmem_limit_bytes=VMEM_LIMIT)


def _layer_norm_rows(r, g, b):
    mu = jnp.mean(r, axis=-1, keepdims=True)
    d = r - mu
    var = jnp.mean(d * d, axis=-1, keepdims=True)
    return d * lax.rsqrt(var + LN_EPS) * g + b


def _mod_kernel(c_ref, w_ref, b_ref, o_ref):
    c = c_ref[...]
    o_ref[0] = _dot(_bf(_silu(c)), _bf(w_ref[0])) + b_ref[0]


def _mod_call(c_pad, w_mod, b_mod):
    depth, d, n = w_mod.shape
    rows = c_pad.shape[0]
    tn = 1024
    return pl.pallas_call(
        _mod_kernel,
        grid=(depth, n // tn),
        in_specs=[
            pl.BlockSpec((rows, d), lambda l, j: (0, 0)),
            pl.BlockSpec((1, d, tn), lambda l, j: (l, 0, j)),
            pl.BlockSpec((1, 1, tn), lambda l, j: (l, 0, j)),
        ],
        out_specs=pl.BlockSpec((1, rows, tn), lambda l, j: (l, 0, j)),
        out_shape=jax.ShapeDtypeStruct((depth, rows, n), F32),
        compiler_params=_params("arbitrary", "arbitrary"),
        name="mod",
    )(c_pad, w_mod, b_mod.reshape(depth, 1, n))


def _mod_spec(layer, which, tpb, d):
    return pl.BlockSpec((None, 1, d),
                        lambda i, *_: ((layer * MOD_ROWS + i // tpb) * N_MOD + which, 0, 0))


def _ln_spec(layer, which, d):
    return pl.BlockSpec((None, 1, d), lambda i, *_: (2 * layer + which, 0, 0))


def _proj_kernel(has_small, x_ref, sh_ref, sc_ref, w_ref, f1_ref, f2_ref, *rest):
    if has_small:
        ws_ref, o_ref, f1b_ref, f2b_ref, os_ref, h_scr = rest
    else:
        o_ref, f1b_ref, f2b_ref, h_scr = rest

    @pl.when(pl.program_id(1) == 0)
    def _():
        h = _bf(x_ref[...] * (1.0 + sc_ref[...]) + sh_ref[...])
        h_scr[...] = h
        if has_small:
            os_ref[...] = _dot(h, ws_ref[...])

    o_ref[...] = _dot(h_scr[...], w_ref[...])
    f1b_ref[...] = _bf(f1_ref[...])
    f2b_ref[...] = _bf(f2_ref[...])


def _proj_tn(nout):
    return next(t for t in (1280, 1024) if nout % t == 0)


def _proj_call(x, modv, layer, w, slot, nout, seq, w_ff1, w_ff2, w_small=None):
    n, d = x.shape
    dff = w_ff1.shape[2]
    tm = min(1024, seq)
    tn = _proj_tn(nout)
    tpb = seq // tm
    cols = nout // tn
    steps = (n // tm) * cols
    nch = next(c for c in (64, 32, 16, 8, 4, 2, 1) if c <= steps)
    r1 = d // nch
    r2 = dff // nch
    chunk = lambda i, j: jnp.minimum(i * cols + j, nch - 1)
    has_small = w_small is not None
    in_specs = [
        pl.BlockSpec((tm, d), lambda i, j: (i, 0)),
        _mod_spec(layer, 0, tpb, d),
        _mod_spec(layer, 1, tpb, d),
        pl.BlockSpec((None, d, tn), lambda i, j: (slot, 0, j)),
        pl.BlockSpec((None, r1, dff), lambda i, j: (layer, chunk(i, j), 0)),
        pl.BlockSpec((None, r2, d), lambda i, j: (layer, chunk(i, j), 0)),
    ]
    out_specs = [pl.BlockSpec((tm, tn), lambda i, j: (i, j)),
                 pl.BlockSpec((r1, dff), lambda i, j: (chunk(i, j), 0)),
                 pl.BlockSpec((r2, d), lambda i, j: (chunk(i, j), 0))]
    out_shape = [jax.ShapeDtypeStruct((n, nout), F32),
                 jax.ShapeDtypeStruct((d, dff), BF16),
                 jax.ShapeDtypeStruct((dff, d), BF16)]
    args = [x, modv, modv, w, w_ff1, w_ff2]
    if has_small:
        ns = w_small.shape[1]
        in_specs.append(pl.BlockSpec((d, ns), lambda i, j: (0, 0)))
        out_specs.append(pl.BlockSpec((tm, ns), lambda i, j: (i, 0)))
        out_shape.append(jax.ShapeDtypeStruct((n, ns), F32))
        args.append(w_small)
    outs = pl.pallas_call(
        functools.partial(_proj_kernel, has_small),
        grid=(n // tm, cols),
        in_specs=in_specs,
        out_specs=out_specs,
        out_shape=out_shape,
        scratch_shapes=[pltpu.VMEM((tm, d), BF16)],
        compiler_params=_params("arbitrary", "arbitrary"),
        name="proj",
    )(*args)
    return tuple(outs) if has_small else (*outs, None)


def _outln_kernel(y_ref, w_ref, x_ref, gt_ref, g_ref, b_ref, o_ref):
    r = DEEPNORM_ALPHA * x_ref[...] + (1.0 + gt_ref[...]) * _dot(y_ref[...], w_ref[...])
    o_ref[...] = _layer_norm_rows(r, g_ref[...], b_ref[...])


def _outln_call(y, w, slot, x, modv, lng, lnb, layer, seq):
    n, d = x.shape
    kdim = y.shape[1]
    tm = min(512, seq)
    tpb = seq // tm
    return pl.pallas_call(
        _outln_kernel,
        grid=(n // tm,),
        in_specs=[
            pl.BlockSpec((tm, kdim), lambda i: (i, 0)),
            pl.BlockSpec((None, kdim, d), lambda i: (slot, 0, 0), pipeline_mode=pl.Buffered(1)),
            pl.BlockSpec((tm, d), lambda i: (i, 0)),
            _mod_spec(layer, 2, tpb, d),
            _ln_spec(layer, 0, d),
            _ln_spec(layer, 0, d),
        ],
        out_specs=pl.BlockSpec((tm, d), lambda i: (i, 0)),
        out_shape=jax.ShapeDtypeStruct((n, d), F32),
        compiler_params=_params("arbitrary"),
        name="outln",
    )(y, w, x, modv, lng, lnb)


def _mlp_kernel(x_ref, sh_ref, sc_ref, gt_ref, w1_ref, w2_ref, g_ref, b_ref, o_ref, h_scr):
    k = pl.program_id(1)

    @pl.when(k == 0)
    def _():
        h_scr[...] = _bf(x_ref[...] * (1.0 + sc_ref[...]) + sh_ref[...])
        o_ref[...] = jnp.zeros_like(o_ref)

    hid = jnp.maximum(_dot(h_scr[...], w1_ref[...]), 0.0)
    o_ref[...] += _dot(_bf(hid * hid), w2_ref[...])

    @pl.when(k == pl.num_programs(1) - 1)
    def _():
        r = DEEPNORM_ALPHA * x_ref[...] + (1.0 + gt_ref[...]) * o_ref[...]
        o_ref[...] = _layer_norm_rows(r, g_ref[...], b_ref[...])


def _mlp_call(x, modv, w1, w2, lng, lnb, layer, seq):
    n, d = x.shape
    dff = w1.shape[1]
    tm = min(512, seq)
    tf = 1024
    tpb = seq // tm
    return pl.pallas_call(
        _mlp_kernel,
        grid=(n // tm, dff // tf),
        in_specs=[
            pl.BlockSpec((tm, d), lambda i, k: (i, 0)),
            _mod_spec(layer, 3, tpb, d),
            _mod_spec(layer, 4, tpb, d),
            _mod_spec(layer, 5, tpb, d),
            pl.BlockSpec((d, tf), lambda i, k: (0, k)),
            pl.BlockSpec((tf, d), lambda i, k: (k, 0)),
            _ln_spec(layer, 1, d),
            _ln_spec(layer, 1, d),
        ],
        out_specs=pl.BlockSpec((tm, d), lambda i, k: (i, 0)),
        out_shape=jax.ShapeDtypeStruct((n, d), F32),
        scratch_shapes=[pltpu.VMEM((tm, d), BF16)],
        compiler_params=_params("arbitrary", "arbitrary"),
        name="mlp",
    )(x, modv, modv, modv, w1, w2, lng, lnb)


def _rglru_kernel(gate_ref, rec_ref, cw_ref, cb_ref, wg_ref, bg_ref, lam_ref, o_ref,
                  xbuf, abuf, bbuf, hcar):
    tm = rec_ref.shape[0]

    @pl.when(pl.program_id(1) == 0)
    def _():
        xbuf[0:SUBLANES, :] = jnp.zeros((SUBLANES, RNN_WIDTH), F32)
        hcar[...] = jnp.zeros_like(hcar)

    xbuf[SUBLANES:SUBLANES + tm, :] = rec_ref[...]
    sp = _softplus(-lam_ref[...])
    for g in range(RNN_GROUPS):
        cs = slice(RNN_GROUP * g, RNN_GROUP * (g + 1))
        xr = cb_ref[:, cs]
        for j in range(CONV_WIDTH):
            r0 = SUBLANES - (CONV_WIDTH - 1) + j
            xr = xr + cw_ref[j:j + 1, cs] * xbuf[r0:r0 + tm, cs]
        pre = _dot(_bf(xr), wg_ref[g]) + bg_ref[g]
        r_gate = jax.nn.sigmoid(pre[:, :RNN_GROUP])
        i_gate = jax.nn.sigmoid(pre[:, RNN_GROUP:])
        log_a = (-LRU_C) * r_gate * sp[:, cs]
        a = jnp.exp(log_a)
        abuf[:, cs] = a
        bbuf[:, cs] = jnp.sqrt(-jnp.tanh(log_a) * (1.0 + a * a)) * (i_gate * xr)
    xbuf[0:SUBLANES, :] = xbuf[tm:tm + SUBLANES, :]

    row = lax.broadcasted_iota(jnp.int32, (SUBLANES, RNN_WIDTH), 0)

    def body(j, carry):
        r0 = pl.multiple_of(j * SUBLANES, SUBLANES)
        a = abuf[pl.ds(r0, SUBLANES), :]
        b = bbuf[pl.ds(r0, SUBLANES), :]
        for k in (1, 2, 4):
            a_s = jnp.where(row >= k, pltpu.roll(a, k, 0), 1.0)
            b_s = jnp.where(row >= k, pltpu.roll(b, k, 0), 0.0)
            b = a * b_s + b
            a = a * a_s
        h = a * carry + b
        bbuf[pl.ds(r0, SUBLANES), :] = h
        return h[SUBLANES - 1:SUBLANES, :]

    hcar[...] = lax.fori_loop(0, tm // SUBLANES, body, hcar[...])
    o_ref[...] = _bf(jax.nn.gelu(gate_ref[...], approximate=True) * bbuf[...])


def _rglru_core_call(u, conv_w, conv_b, wg, bg, lam, batch, seq):
    n = u.shape[0]
    tm = min(256, seq)
    tpb = seq // tm
    r = RNN_WIDTH
    const2 = lambda b, t: (0, 0)
    return pl.pallas_call(
        _rglru_kernel,
        grid=(batch, tpb),
        in_specs=[
            pl.BlockSpec((tm, r), lambda b, t: (b * tpb + t, 0)),
            pl.BlockSpec((tm, r), lambda b, t: (b * tpb + t, 1)),
            pl.BlockSpec((CONV_WIDTH, r), const2),
            pl.BlockSpec((1, r), const2),
            pl.BlockSpec((RNN_GROUPS, RNN_GROUP, 2 * RNN_GROUP), lambda b, t: (0, 0, 0)),
            pl.BlockSpec((RNN_GROUPS, 1, 2 * RNN_GROUP), lambda b, t: (0, 0, 0)),
            pl.BlockSpec((1, r), const2),
        ],
        out_specs=pl.BlockSpec((tm, r), lambda b, t: (b * tpb + t, 0)),
        out_shape=jax.ShapeDtypeStruct((n, r), BF16),
        scratch_shapes=[
            pltpu.VMEM((tm + SUBLANES, r), F32),
            pltpu.VMEM((tm, r), F32),
            pltpu.VMEM((tm, r), F32),
            pltpu.VMEM((1, r), F32),
        ],
        compiler_params=_params("arbitrary", "arbitrary"),
        name="rglru_core",
    )(u, u, conv_w, conv_b, wg, bg, lam)


def _gdn_kernel(qkv_ref, z_ref, br_ref, ar_ref, cw_ref, alog_ref, dtb_ref, nw_ref, o_ref,
                xbuf, qs, ks, vs, gcb, btb, gct, s_scr, o_scr):
    c = CHUNK
    hd = GDN_HEAD_DIM

    @pl.when(pl.program_id(1) == 0)
    def _():
        xbuf[0:SUBLANES, :] = jnp.zeros((SUBLANES, GDN_CONV_DIM), F32)
        s_scr[...] = jnp.zeros_like(s_scr)

    xbuf[SUBLANES:SUBLANES + c, :] = qkv_ref[...]

    beta = jax.nn.sigmoid(br_ref[...])
    g = -jnp.exp(alog_ref[...]) * _softplus(ar_ref[...] + dtb_ref[...])
    ti64 = lax.broadcasted_iota(jnp.int32, (c, c), 0)
    tj64 = lax.broadcasted_iota(jnp.int32, (c, c), 1)
    tri = jnp.where(ti64 >= tj64, 1.0, 0.0).astype(BF16)
    gc = _dot_mask_lhs(tri, g)
    g_t = jnp.concatenate([g, jnp.zeros_like(g)], axis=0).T
    ut = lax.broadcasted_iota(jnp.int32, (2 * c, 2 * c), 0)
    uj = lax.broadcasted_iota(jnp.int32, (2 * c, 2 * c), 1) & (c - 1)
    upper2 = jnp.where((ut <= uj) & (ut < c), 1.0, 0.0).astype(BF16)
    gct[...] = _dot_mask_rhs(g_t, upper2)
    for h in range(GDN_V_HEADS):
        gcb[h] = jnp.broadcast_to(gc[:, h:h + 1], (c, LANES))
        btb[h] = jnp.broadcast_to(beta[:, h:h + 1], (c, LANES))

    def conv_silu(col0, width):
        acc = None
        for j in range(CONV_WIDTH):
            r0 = SUBLANES - (CONV_WIDTH - 1) + j
            term = cw_ref[j:j + 1, col0:col0 + width] * xbuf[r0:r0 + c, col0:col0 + width]
            acc = term if acc is None else acc + term
        return _silu(acc)

    for p in range(GDN_PAIRS):
        q = conv_silu(hd * p, hd)
        q = q * lax.rsqrt(jnp.sum(q * q, axis=-1, keepdims=True) + RMS_EPS) * (hd ** -0.5)
        k = conv_silu(GDN_KEY_DIM + hd * p, hd)
        k = k * lax.rsqrt(jnp.sum(k * k, axis=-1, keepdims=True) + RMS_EPS)
        qs[p] = q
        ks[p] = k
        vs[p] = conv_silu(2 * GDN_KEY_DIM + 2 * hd * p, 2 * hd)
    xbuf[0:SUBLANES, :] = xbuf[c:c + SUBLANES, :]

    ti = lax.broadcasted_iota(jnp.int32, (c, LANES), 0)
    lane = lax.broadcasted_iota(jnp.int32, (c, LANES), 1)
    left = lane < c
    tj = lane & (c - 1)
    left_row = lax.broadcasted_iota(jnp.int32, (1, LANES), 1) < c
    eye = jnp.where(ti == tj, 1.0, 0.0)
    left2 = lax.broadcasted_iota(jnp.int32, (c, 2 * hd), 1) < hd
    left2s = lax.broadcasted_iota(jnp.int32, (hd, 2 * hd), 1) < hd
    zero_tile = jnp.zeros((c, hd), F32)

    def block_diag(x, msk):
        return jnp.concatenate([jnp.where(msk, x, 0.0), jnp.where(msk, 0.0, x)], axis=0)

    def gc_row(p):
        return jnp.where(left_row, gct[2 * p:2 * p + 1, :], gct[2 * p + 1:2 * p + 2, :])

    for g0 in range(0, GDN_PAIRS, GDN_GROUP):
        grp = range(g0, g0 + GDN_GROUP)
        kq, a_qk, w_pow, t_inv, wu, r1, r2, v_new = {}, {}, {}, {}, {}, {}, {}, {}
        for p in grp:
            k = ks[p]
            kq[p] = _dot_nt(_bf(jnp.concatenate([k, qs[p]], axis=0)),
                            _bf(jnp.concatenate([k, k], axis=0)))
        for p in grp:
            col = jnp.where(left, gcb[2 * p], gcb[2 * p + 1])
            decay = jnp.where(ti >= tj, jnp.exp(jnp.minimum(col - gc_row(p), 0.0)), 0.0)
            a_kk = (jnp.where(ti > tj, kq[p][:c] * decay, 0.0)
                    * jnp.where(left, btb[2 * p], btb[2 * p + 1]))
            a_qk[p] = kq[p][c:] * decay
            w_pow[p] = _dot(_bf(a_kk), _bf(block_diag(a_kk, left)))
            t_inv[p] = eye - a_kk
        for _ in range(4):
            for p in grp:
                r = _dot(_bf(jnp.concatenate([w_pow[p], t_inv[p]], axis=0)),
                         _bf(block_diag(w_pow[p], left)))
                w_pow[p] = r[:c]
                t_inv[p] = t_inv[p] + r[c:]
        for p in grp:
            t_inv[p] = t_inv[p] + _dot(_bf(t_inv[p]), _bf(block_diag(w_pow[p], left)))
        for p in grp:
            k = ks[p]
            bt0 = btb[2 * p]
            bt1 = btb[2 * p + 1]
            v_ss = vs[p]
            rhs = jnp.concatenate([
                jnp.concatenate([k * (bt0 * jnp.exp(gcb[2 * p])), zero_tile,
                                 v_ss[:, :hd] * bt0, zero_tile], axis=1),
                jnp.concatenate([zero_tile, k * (bt1 * jnp.exp(gcb[2 * p + 1])),
                                 zero_tile, v_ss[:, hd:] * bt1], axis=1)], axis=0)
            wu[p] = _dot(_bf(t_inv[p]), _bf(rhs))
        for p in grp:
            q = qs[p]
            q_dec = jnp.concatenate([q * jnp.exp(gcb[2 * p]), q * jnp.exp(gcb[2 * p + 1])], axis=1)
            r1[p] = _dot(_bf(jnp.concatenate([wu[p][:, :2 * hd], q_dec], axis=0)),
                         _bf(block_diag(s_scr[p], left2s)))
        for p in grp:
            v_new[p] = wu[p][:, 2 * hd:] - r1[p][:c]
            k = ks[p]
            gl = jnp.where(left_row, gcb[2 * p][c - 1:c, :], gcb[2 * p + 1][c - 1:c, :])
            k_tail = jnp.concatenate([k, k], axis=0).T * jnp.exp(gl - gc_row(p))
            r2[p] = _dot(_bf(jnp.concatenate([a_qk[p], k_tail], axis=0)),
                         _bf(block_diag(v_new[p], left2)))
        for p in grp:
            o_scr[p] = r1[p][c:] + r2[p][:c]
            cd = jnp.concatenate([jnp.exp(gcb[2 * p][c - 1:c, :]),
                                  jnp.exp(gcb[2 * p + 1][c - 1:c, :])], axis=1)
            s_scr[p] = s_scr[p] * cd + r2[p][c:]

    nw = nw_ref[...]
    for p in range(GDN_PAIRS):
        o_ss = o_scr[p]
        for s in range(2):
            cs = slice(2 * hd * p + hd * s, 2 * hd * p + hd * (s + 1))
            o = o_ss[:, hd * s:hd * (s + 1)]
            nrm = o * lax.rsqrt(jnp.mean(o * o, axis=-1, keepdims=True) + RMS_EPS)
            o_ref[:, cs] = _bf(nrm * nw * _silu(z_ref[:, cs]))


def _gdn_core_call(u, u_gate, conv_w, a_log, dt_bias, norm_w, batch, seq):
    n = u.shape[0]
    c = CHUNK
    cpb = seq // c
    hd = GDN_HEAD_DIM
    row = lambda b, t: b * cpb + t
    const2 = lambda b, t: (0, 0)
    return pl.pallas_call(
        _gdn_kernel,
        grid=(batch, cpb),
        in_specs=[
            pl.BlockSpec((c, GDN_CONV_DIM), lambda b, t: (row(b, t), 0)),
            pl.BlockSpec((c, GDN_VALUE_DIM), lambda b, t: (row(b, t), GDN_CONV_DIM // GDN_VALUE_DIM)),
            pl.BlockSpec((c, LANES), lambda b, t: (row(b, t), 0)),
            pl.BlockSpec((c, LANES), lambda b, t: (row(b, t), 1)),
            pl.BlockSpec((CONV_WIDTH, GDN_CONV_DIM), const2),
            pl.BlockSpec((1, LANES), const2),
            pl.BlockSpec((1, LANES), const2),
            pl.BlockSpec((1, hd), const2),
        ],
        out_specs=pl.BlockSpec((c, GDN_VALUE_DIM), lambda b, t: (row(b, t), 0)),
        out_shape=jax.ShapeDtypeStruct((n, GDN_VALUE_DIM), BF16),
        scratch_shapes=[
            pltpu.VMEM((c + SUBLANES, GDN_CONV_DIM), F32),
            pltpu.VMEM((GDN_PAIRS, c, hd), F32),
            pltpu.VMEM((GDN_PAIRS, c, hd), F32),
            pltpu.VMEM((GDN_PAIRS, c, 2 * hd), F32),
            pltpu.VMEM((GDN_V_HEADS, c, LANES), F32),
            pltpu.VMEM((GDN_V_HEADS, c, LANES), F32),
            pltpu.VMEM((2 * c, 2 * c), F32),
            pltpu.VMEM((GDN_PAIRS, hd, 2 * hd), F32),
            pltpu.VMEM((GDN_PAIRS, c, 2 * hd), F32),
        ],
        compiler_params=_params("arbitrary", "arbitrary"),
        name="gdn_core",
    )(u, u, u_gate, u_gate, conv_w, a_log, dt_bias, norm_w)


def _gla_kernel(q_ref, k_ref, v_ref, z_ref, al_ref, wup_ref, bal_ref, nw_ref, o_ref, s_scr, bc_scr):
    c = CHUNK
    dk = GLA_HEAD_K
    dv = GLA_HEAD_V

    nb = q_ref.shape[0]

    @pl.when(pl.program_id(0) == 0)
    def _():
        s_scr[...] = jnp.zeros_like(s_scr)

    ti = lax.broadcasted_iota(jnp.int32, (c, c), 0)
    tj = lax.broadcasted_iota(jnp.int32, (c, c), 1)
    causal = ti >= tj
    tri = jnp.where(causal, 1.0, 0.0).astype(BF16)
    nw = nw_ref[...]
    chains = [(b, h) for b in range(nb) for h in range(GLA_HEADS)]
    pre = {}
    for b in range(nb):
        pre[b] = _dot(_bf(al_ref[b]), wup_ref[...]) + bal_ref[...]
    for b in range(nb):
        log_alpha = (-1.0 / GLA_GATE_NORMALIZER) * _softplus(-pre[b])
        bc_scr[b] = _dot_mask_lhs(tri, log_alpha)
    q_dec, a_qk, o = {}, {}, {}
    for b, h in chains:
        ks_ = slice(dk * h, dk * (h + 1))
        bcum = bc_scr[b, :, ks_]
        q_dec[b, h] = _bf(q_ref[b, :, ks_] * (dk ** -0.5) * jnp.exp(bcum))
        a_qk[b, h] = _dot_nt(q_dec[b, h], _bf(k_ref[b, :, ks_] * jnp.exp(-bcum)))
    for b, h in chains:
        v = _bf(v_ref[b, :, dv * h:dv * (h + 1)])
        s_t = s_scr[b * GLA_HEADS + h]
        o[b, h] = (_dot_nt(q_dec[b, h], _bf(s_t))
                   + _dot(_bf(jnp.where(causal, a_qk[b, h], 0.0)), v))
    for b, h in chains:
        ks_ = slice(dk * h, dk * (h + 1))
        bcum = bc_scr[b, :, ks_]
        b_last = bcum[c - 1:c, :]
        k_tail = _bf(k_ref[b, :, ks_] * jnp.exp(b_last - bcum))
        i = b * GLA_HEADS + h
        s_scr[i] = s_scr[i] * jnp.exp(b_last) + _dot_tn(_bf(v_ref[b, :, dv * h:dv * (h + 1)]), k_tail)
    for b, h in chains:
        vs_ = slice(dv * h, dv * (h + 1))
        oh = o[b, h]
        nrm = oh * lax.rsqrt(jnp.mean(oh * oh, axis=-1, keepdims=True) + RMS_EPS)
        o_ref[b, :, vs_] = _bf(nrm * nw * _silu(z_ref[b, :, vs_]))


def _gla_core_call(u, u_gate, w_up, b_alpha, norm_w, batch, seq):
    c = CHUNK
    u3 = u.reshape(batch, seq, u.shape[1])
    g3 = u_gate.reshape(batch, seq, u_gate.shape[1])
    const2 = lambda t: (0, 0)
    out = pl.pallas_call(
        _gla_kernel,
        grid=(seq // c,),
        in_specs=[
            pl.BlockSpec((batch, c, GLA_KEY_DIM), lambda t: (0, t, 0)),
            pl.BlockSpec((batch, c, GLA_KEY_DIM), lambda t: (0, t, 1)),
            pl.BlockSpec((batch, c, GLA_VALUE_DIM), lambda t: (0, t, 1)),
            pl.BlockSpec((batch, c, GLA_VALUE_DIM), lambda t: (0, t, 2)),
            pl.BlockSpec((batch, c, LANES), lambda t: (0, t, 0)),
            pl.BlockSpec((LANES, GLA_KEY_DIM), const2),
            pl.BlockSpec((1, GLA_KEY_DIM), const2),
            pl.BlockSpec((1, GLA_HEAD_V), const2),
        ],
        out_specs=pl.BlockSpec((batch, c, GLA_VALUE_DIM), lambda t: (0, t, 0)),
        out_shape=jax.ShapeDtypeStruct((batch, seq, GLA_VALUE_DIM), BF16),
        scratch_shapes=[pltpu.VMEM((batch * GLA_HEADS, GLA_HEAD_V, GLA_HEAD_K), F32),
                        pltpu.VMEM((batch, c, GLA_KEY_DIM), F32)],
        compiler_params=_params("arbitrary"),
        name="gla_core",
    )(u3, u3, u3, u3, g3, w_up, b_alpha, norm_w)
    return out.reshape(batch * seq, GLA_VALUE_DIM)


def _pad_cols(w, total):
    return jnp.pad(w, ((0, 0), (0, total - w.shape[1])))


def _rglru_gate_weights(w_rgate, b_rgate, w_igate, b_igate):
    per = RNN_GROUP // RNN_BLOCK_DIM
    eye = jnp.eye(per, dtype=F32)

    def dense(w):
        w = w.reshape(RNN_GROUPS, per, RNN_BLOCK_DIM, RNN_BLOCK_DIM)
        return jnp.einsum("gjde,jk->gjdke", w, eye).reshape(RNN_GROUPS, RNN_GROUP, RNN_GROUP)

    wg = jnp.concatenate([dense(w_rgate), dense(w_igate)], axis=-1).astype(BF16)
    bg = jnp.concatenate([b_rgate.reshape(RNN_GROUPS, 1, RNN_GROUP),
                          b_igate.reshape(RNN_GROUPS, 1, RNN_GROUP)], axis=-1)
    return wg, bg


def _gdn_gate_weights(w_in):
    main = GDN_CONV_DIM + GDN_VALUE_DIM
    b_w = _pad_cols(w_in[:, main:main + GDN_V_HEADS], LANES)
    a_w = _pad_cols(w_in[:, main + GDN_V_HEADS:main + 2 * GDN_V_HEADS], LANES)
    return jnp.concatenate([b_w, a_w], axis=1)


def _gla_gate_weights(w_in):
    main = 2 * GLA_KEY_DIM + 2 * GLA_VALUE_DIM
    return _pad_cols(w_in[:, main:main + GLA_GATE_RANK], LANES)


def kernel(x, c, ln_g, ln_b, w_mod, b_mod, w_ff1, w_ff2, rglru_w_in, rglru_conv_w, rglru_conv_b, rglru_w_rgate, rglru_b_rgate, rglru_w_igate, rglru_b_igate, rglru_lambda, rglru_w_out, gdn_w_in, gdn_conv_w, gdn_a_log, gdn_dt_bias, gdn_norm_w, gdn_w_out, gla_w_in, gla_w_alpha_up, gla_b_alpha, gla_norm_w, gla_w_out):
    batch, seq, d = x.shape
    depth = w_mod.shape[0]
    n = batch * seq
    xf = x.reshape(n, d)

    assert batch <= MOD_ROWS
    c_pad = jnp.pad(c, ((0, MOD_ROWS - batch), (0, 0)))
    modv = _mod_call(c_pad, w_mod, b_mod).reshape(depth * MOD_ROWS * N_MOD, 1, d)
    lng = ln_g.reshape(depth * 2, 1, d)
    lnb = ln_b.reshape(depth * 2, 1, d)
    rglru_w_in_b = rglru_w_in.astype(BF16)
    gdn_w_in_b = gdn_w_in.astype(BF16)
    gla_w_in_b = gla_w_in.astype(BF16)
    rglru_w_out_b = rglru_w_out.astype(BF16)
    gdn_w_out_b = gdn_w_out.astype(BF16)
    gla_w_out_b = gla_w_out.astype(BF16)
    gdn_main = GDN_CONV_DIM + GDN_VALUE_DIM
    gla_main = 2 * GLA_KEY_DIM + 2 * GLA_VALUE_DIM

    for i in range(depth):
        kind, slot = i % N_MIXERS, i // N_MIXERS
        if kind == 0:
            u, w1b, w2b, _ = _proj_call(xf, modv, i, rglru_w_in_b, slot, 2 * RNN_WIDTH, seq, w_ff1, w_ff2)
            wg, bg = _rglru_gate_weights(rglru_w_rgate[slot], rglru_b_rgate[slot],
                                         rglru_w_igate[slot], rglru_b_igate[slot])
            y = _rglru_core_call(u, rglru_conv_w[slot], rglru_conv_b[slot][None], wg, bg,
                                 rglru_lambda[slot][None], batch, seq)
            w_out = rglru_w_out_b
        elif kind == 1:
            u, w1b, w2b, u_gate = _proj_call(xf, modv, i, gdn_w_in_b, slot, gdn_main, seq, w_ff1, w_ff2,
                                             w_small=_gdn_gate_weights(gdn_w_in_b[slot]))
            y = _gdn_core_call(u, u_gate, gdn_conv_w[slot],
                               _pad_cols(gdn_a_log[slot][None], LANES),
                               _pad_cols(gdn_dt_bias[slot][None], LANES),
                               gdn_norm_w[slot][None], batch, seq)
            w_out = gdn_w_out_b
        else:
            u, w1b, w2b, u_gate = _proj_call(xf, modv, i, gla_w_in_b, slot, gla_main, seq, w_ff1, w_ff2,
                                             w_small=_gla_gate_weights(gla_w_in_b[slot]))
            w_up = jnp.pad(gla_w_alpha_up[slot], ((0, LANES - GLA_GATE_RANK), (0, 0))).astype(BF16)
            y = _gla_core_call(u, u_gate, w_up, gla_b_alpha[slot][None], gla_norm_w[slot][None], batch, seq)
            w_out = gla_w_out_b
        xf = _outln_call(y, w_out, slot, xf, modv, lng, lnb, i, seq)
        xf = _mlp_call(xf, modv, w1b, w2b, lng, lnb, i, seq)
    return xf.reshape(batch, seq, d)
```

```python
import functools

import jax
import jax.numpy as jnp
from jax import lax
from jax.experimental import pallas as pl
from jax.experimental.pallas import tpu as pltpu

F32 = jnp.float32
BF16 = jnp.bfloat16

D_MODEL = 2048
DEPTH = 4
N_MIXERS = 3
DEEPNORM_ALPHA = float((2 * DEPTH) ** 0.25)
LN_EPS = 1e-5
RMS_EPS = 1e-6
D_FF = 4 * D_MODEL
N_MOD = 6
MOD_ROWS = 8
CONV_WIDTH = 4

RNN_WIDTH = (5 * D_MODEL) // 4
RNN_BLOCKS = 16
RNN_BLOCK_DIM = RNN_WIDTH // RNN_BLOCKS
RNN_GROUP = 640
RNN_GROUPS = RNN_WIDTH // RNN_GROUP
LRU_C = 8.0

GDN_HEAD_DIM = 128
GDN_QK_HEADS = D_MODEL // GDN_HEAD_DIM
GDN_V_HEADS = 2 * GDN_QK_HEADS
GDN_KEY_DIM = GDN_QK_HEADS * GDN_HEAD_DIM
GDN_VALUE_DIM = GDN_V_HEADS * GDN_HEAD_DIM
GDN_CONV_DIM = 2 * GDN_KEY_DIM + GDN_VALUE_DIM
GDN_PAIRS = GDN_QK_HEADS
GDN_GROUP = 8

GLA_HEADS = 4
GLA_KEY_DIM = D_MODEL // 2
GLA_VALUE_DIM = D_MODEL
GLA_HEAD_K = GLA_KEY_DIM // GLA_HEADS
GLA_HEAD_V = GLA_VALUE_DIM // GLA_HEADS
GLA_GATE_RANK = 16
GLA_GATE_NORMALIZER = 16.0

CHUNK = 64
LANES = 128
SUBLANES = 8
VMEM_LIMIT = 56 * 1024 * 1024


def _dot(a, b):
    return jnp.dot(a, b, preferred_element_type=F32)


def _dot_nt(a, b):
    return lax.dot_general(a, b, (((1,), (1,)), ((), ())), preferred_element_type=F32)


def _dot_tn(a, b):
    return lax.dot_general(a, b, (((0,), (0,)), ((), ())), preferred_element_type=F32)


def _bf(x):
    return x.astype(BF16)


def _split3(x):
    hi = _bf(x)
    r1 = x - hi.astype(F32)
    mid = _bf(r1)
    lo = _bf(r1 - mid.astype(F32))
    return hi, mid, lo


def _dot_mask_lhs(m, x):
    hi, mid, lo = _split3(x)
    return _dot(m, hi) + _dot(m, mid) + _dot(m, lo)


def _dot_mask_rhs(x, m):
    hi, mid, lo = _split3(x)
    return _dot(hi, m) + _dot(mid, m) + _dot(lo, m)


def _softplus(x):
    return jnp.maximum(x, 0.0) + jnp.log1p(jnp.exp(-jnp.abs(x)))


def _silu(x):
    return x * jax.nn.sigmoid(x)


def _params(*sem):
    return pltpu.CompilerParams(dimension_semantics=sem, vmem_limit_bytes=VMEM_LIMIT)


def _causal_conv(prev, x, cw_ref, cs):
    x_ext = jnp.concatenate([prev, x], axis=0)
    acc = cw_ref[CONV_WIDTH - 1:CONV_WIDTH, cs] * x
    for s in range(1, CONV_WIDTH):
        shifted = pltpu.roll(x_ext, s, 0)[SUBLANES:]
        acc = acc + cw_ref[CONV_WIDTH - 1 - s:CONV_WIDTH - s, cs] * shifted
    return acc


def _layer_norm_rows(r, g, b):
    mu = jnp.mean(r, axis=-1, keepdims=True)
    d = r - mu
    var = jnp.mean(d * d, axis=-1, keepdims=True)
    return d * lax.rsqrt(var + LN_EPS) * g + b


def _mod_kernel(c_ref, w_ref, b_ref, o_ref):
    c = c_ref[...]
    o_ref[0] = _dot(_bf(_silu(c)), _bf(w_ref[0])) + b_ref[0]


def _mod_call(c_pad, w_mod, b_mod):
    depth, d, n = w_mod.shape
    rows = c_pad.shape[0]
    tn = 1024
    return pl.pallas_call(
        _mod_kernel,
        grid=(depth, n // tn),
        in_specs=[
            pl.BlockSpec((rows, d), lambda l, j: (0, 0)),
            pl.BlockSpec((1, d, tn), lambda l, j: (l, 0, j)),
            pl.BlockSpec((1, 1, tn), lambda l, j: (l, 0, j)),
        ],
        out_specs=pl.BlockSpec((1, rows, tn), lambda l, j: (l, 0, j)),
        out_shape=jax.ShapeDtypeStruct((depth, rows, n), F32),
        compiler_params=_params("arbitrary", "arbitrary"),
        name="mod",
    )(c_pad, w_mod, b_mod.reshape(depth, 1, n))


def _mod_spec(layer, which, tpb, d):
    return pl.BlockSpec((None, 1, d),
                        lambda i, *_: ((layer * MOD_ROWS + i // tpb) * N_MOD + which, 0, 0))


def _ln_spec(layer, which, d):
    return pl.BlockSpec((None, 1, d), lambda i, *_: (2 * layer + which, 0, 0))


def _proj_kernel(has_small, x_ref, sh_ref, sc_ref, w_ref, *rest):
    if has_small:
        ws_ref, o_ref, os_ref, h_scr = rest
    else:
        o_ref, h_scr = rest

    @pl.when(pl.program_id(1) == 0)
    def _():
        h = _bf(x_ref[...] * (1.0 + sc_ref[...]) + sh_ref[...])
        h_scr[...] = h
        if has_small:
            os_ref[...] = _dot(h, ws_ref[...])

    o_ref[...] = _dot(h_scr[...], w_ref[...])


def _proj_tn(nout):
    return next(t for t in (1280, 1024) if nout % t == 0)


def _proj_call(x, modv, layer, w, slot, nout, seq, w_small=None):
    n, d = x.shape
    tm = min(1024, seq)
    tn = _proj_tn(nout)
    tpb = seq // tm
    has_small = w_small is not None
    in_specs = [
        pl.BlockSpec((tm, d), lambda i, j: (i, 0)),
        _mod_spec(layer, 0, tpb, d),
        _mod_spec(layer, 1, tpb, d),
        pl.BlockSpec((None, d, tn), lambda i, j: (slot, 0, j)),
    ]
    out_specs = [pl.BlockSpec((tm, tn), lambda i, j: (i, j))]
    out_shape = [jax.ShapeDtypeStruct((n, nout), F32)]
    args = [x, modv, modv, w]
    if has_small:
        ns = w_small.shape[1]
        in_specs.append(pl.BlockSpec((d, ns), lambda i, j: (0, 0)))
        out_specs.append(pl.BlockSpec((tm, ns), lambda i, j: (i, 0)))
        out_shape.append(jax.ShapeDtypeStruct((n, ns), F32))
        args.append(w_small)
    outs = pl.pallas_call(
        functools.partial(_proj_kernel, has_small),
        grid=(n // tm, nout // tn),
        in_specs=in_specs,
        out_specs=out_specs,
        out_shape=out_shape,
        scratch_shapes=[pltpu.VMEM((tm, d), BF16)],
        compiler_params=_params("arbitrary", "arbitrary"),
        name="proj",
    )(*args)
    return tuple(outs) if has_small else (outs[0], None)


class _CastJob:
    def __init__(self, layer, w_ff1, w_ff2, steps, step_of):
        _, d, dff = w_ff1.shape
        nch = next(c for c in (128, 64, 32, 16, 8, 4, 2, 1) if c <= steps)
        r1, r2 = d // nch, dff // nch
        chunk = lambda *idx: jnp.minimum(step_of(*idx), nch - 1)
        self.args = [w_ff1, w_ff2]
        self.in_specs = [pl.BlockSpec((None, r1, dff), lambda *idx: (layer, chunk(*idx), 0)),
                         pl.BlockSpec((None, r2, d), lambda *idx: (layer, chunk(*idx), 0))]
        self.out_specs = [pl.BlockSpec((r1, dff), lambda *idx: (chunk(*idx), 0)),
                          pl.BlockSpec((r2, d), lambda *idx: (chunk(*idx), 0))]
        self.out_shape = [jax.ShapeDtypeStruct((d, dff), BF16), jax.ShapeDtypeStruct((dff, d), BF16)]

    @staticmethod
    def run(f1_ref, f2_ref, f1b_ref, f2b_ref):
        f1b_ref[...] = _bf(f1_ref[...])
        f2b_ref[...] = _bf(f2_ref[...])


def _outln_kernel(y_ref, w_ref, x_ref, gt_ref, g_ref, b_ref, o_ref):
    r = DEEPNORM_ALPHA * x_ref[...] + (1.0 + gt_ref[...]) * _dot(y_ref[...], w_ref[...])
    o_ref[...] = _layer_norm_rows(r, g_ref[...], b_ref[...])


def _outln_call(y, w, slot, x, modv, lng, lnb, layer, seq):
    n, d = x.shape
    kdim = y.shape[1]
    tm = min(512, seq)
    tpb = seq // tm
    return pl.pallas_call(
        _outln_kernel,
        grid=(n // tm,),
        in_specs=[
            pl.BlockSpec((tm, kdim), lambda i: (i, 0)),
            pl.BlockSpec((None, kdim, d), lambda i: (slot, 0, 0), pipeline_mode=pl.Buffered(1)),
            pl.BlockSpec((tm, d), lambda i: (i, 0)),
            _mod_spec(layer, 2, tpb, d),
            _ln_spec(layer, 0, d),
            _ln_spec(layer, 0, d),
        ],
        out_specs=pl.BlockSpec((tm, d), lambda i: (i, 0)),
        out_shape=jax.ShapeDtypeStruct((n, d), F32),
        compiler_params=_params("arbitrary"),
        name="outln",
    )(y, w, x, modv, lng, lnb)


def _mlp_kernel(x_ref, sh_ref, sc_ref, gt_ref, w1_ref, w2_ref, g_ref, b_ref, o_ref, h_scr):
    k = pl.program_id(1)

    @pl.when(k == 0)
    def _():
        h_scr[...] = _bf(x_ref[...] * (1.0 + sc_ref[...]) + sh_ref[...])
        o_ref[...] = jnp.zeros_like(o_ref)

    hid = jnp.maximum(_dot(h_scr[...], w1_ref[...]), 0.0)
    o_ref[...] += _dot(_bf(hid * hid), w2_ref[...])

    @pl.when(k == pl.num_programs(1) - 1)
    def _():
        r = DEEPNORM_ALPHA * x_ref[...] + (1.0 + gt_ref[...]) * o_ref[...]
        o_ref[...] = _layer_norm_rows(r, g_ref[...], b_ref[...])


def _mlp_call(x, modv, w1, w2, lng, lnb, layer, seq):
    n, d = x.shape
    dff = w1.shape[1]
    tm = min(512, seq)
    tf = 1024
    tpb = seq // tm
    return pl.pallas_call(
        _mlp_kernel,
        grid=(n // tm, dff // tf),
        in_specs=[
            pl.BlockSpec((tm, d), lambda i, k: (i, 0)),
            _mod_spec(layer, 3, tpb, d),
            _mod_spec(layer, 4, tpb, d),
            _mod_spec(layer, 5, tpb, d),
            pl.BlockSpec((d, tf), lambda i, k: (0, k)),
            pl.BlockSpec((tf, d), lambda i, k: (k, 0)),
            _ln_spec(layer, 1, d),
            _ln_spec(layer, 1, d),
        ],
        out_specs=pl.BlockSpec((tm, d), lambda i, k: (i, 0)),
        out_shape=jax.ShapeDtypeStruct((n, d), F32),
        scratch_shapes=[pltpu.VMEM((tm, d), BF16)],
        compiler_params=_params("arbitrary", "arbitrary"),
        name="mlp",
    )(x, modv, modv, modv, w1, w2, lng, lnb)


def _rglru_kernel(gate_ref, rec_ref, cw_ref, cb_ref, wg_ref, bg_ref, lam_ref, f1_ref, f2_ref,
                  o_ref, f1b_ref, f2b_ref, halo, abuf, bbuf, hcar):
    tm = rec_ref.shape[0]
    _CastJob.run(f1_ref, f2_ref, f1b_ref, f2b_ref)

    @pl.when(pl.program_id(1) == 0)
    def _():
        halo[...] = jnp.zeros_like(halo)
        hcar[...] = jnp.zeros_like(hcar)

    sp = _softplus(-lam_ref[...])
    for g in range(RNN_GROUPS):
        cs = slice(RNN_GROUP * g, RNN_GROUP * (g + 1))
        xr = _causal_conv(halo[:, cs], rec_ref[:, cs], cw_ref, cs) + cb_ref[:, cs]
        pre = _dot(_bf(xr), wg_ref[g]) + bg_ref[g]
        r_gate = jax.nn.sigmoid(pre[:, :RNN_GROUP])
        i_gate = jax.nn.sigmoid(pre[:, RNN_GROUP:])
        neg_log_a = LRU_C * r_gate * sp[:, cs]
        a = jnp.exp(-neg_log_a)
        abuf[:, cs] = a
        bbuf[:, cs] = jnp.sqrt(jnp.tanh(neg_log_a) * (1.0 + a * a)) * (i_gate * xr)
    halo[...] = rec_ref[tm - SUBLANES:tm, :]

    row = lax.broadcasted_iota(jnp.int32, (SUBLANES, RNN_WIDTH), 0)

    def body(j, carry):
        r0 = pl.multiple_of(j * SUBLANES, SUBLANES)
        a = abuf[pl.ds(r0, SUBLANES), :]
        b = bbuf[pl.ds(r0, SUBLANES), :]
        for k in (1, 2, 4):
            a_s = jnp.where(row >= k, pltpu.roll(a, k, 0), 1.0)
            b_s = jnp.where(row >= k, pltpu.roll(b, k, 0), 0.0)
            b = a * b_s + b
            a = a * a_s
        h = a * carry + b
        bbuf[pl.ds(r0, SUBLANES), :] = h
        return h[SUBLANES - 1:SUBLANES, :]

    hcar[...] = lax.fori_loop(0, tm // SUBLANES, body, hcar[...])
    o_ref[...] = _bf(jax.nn.gelu(gate_ref[...], approximate=True) * bbuf[...])


def _rglru_core_call(u, conv_w, conv_b, wg, bg, lam, batch, seq, layer, w_ff1, w_ff2):
    n = u.shape[0]
    tm = min(256, seq)
    tpb = seq // tm
    r = RNN_WIDTH
    const2 = lambda b, t: (0, 0)
    job = _CastJob(layer, w_ff1, w_ff2, batch * tpb, lambda b, t: b * tpb + t)
    return pl.pallas_call(
        _rglru_kernel,
        grid=(batch, tpb),
        in_specs=[
            pl.BlockSpec((tm, r), lambda b, t: (b * tpb + t, 0)),
            pl.BlockSpec((tm, r), lambda b, t: (b * tpb + t, 1)),
            pl.BlockSpec((CONV_WIDTH, r), const2),
            pl.BlockSpec((1, r), const2),
            pl.BlockSpec((RNN_GROUPS, RNN_GROUP, 2 * RNN_GROUP), lambda b, t: (0, 0, 0)),
            pl.BlockSpec((RNN_GROUPS, 1, 2 * RNN_GROUP), lambda b, t: (0, 0, 0)),
            pl.BlockSpec((1, r), const2),
        ] + job.in_specs,
        out_specs=[pl.BlockSpec((tm, r), lambda b, t: (b * tpb + t, 0))] + job.out_specs,
        out_shape=[jax.ShapeDtypeStruct((n, r), BF16)] + job.out_shape,
        scratch_shapes=[
            pltpu.VMEM((SUBLANES, r), F32),
            pltpu.VMEM((tm, r), F32),
            pltpu.VMEM((tm, r), F32),
            pltpu.VMEM((1, r), F32),
        ],
        compiler_params=_params("arbitrary", "arbitrary"),
        name="rglru_core",
    )(u, u, conv_w, conv_b, wg, bg, lam, *job.args)


def _gdn_kernel(qkv_ref, z_ref, br_ref, ar_ref, cw_ref, alog_ref, dtb_ref, nw_ref, f1_ref, f2_ref,
                o_ref, f1b_ref, f2b_ref, halo, qs, ks, vs, gcb, btb, gct, s_scr, o_scr):
    c = CHUNK
    hd = GDN_HEAD_DIM
    _CastJob.run(f1_ref, f2_ref, f1b_ref, f2b_ref)

    @pl.when(pl.program_id(1) == 0)
    def _():
        halo[...] = jnp.zeros_like(halo)
        s_scr[...] = jnp.zeros_like(s_scr)

    beta = jax.nn.sigmoid(br_ref[...])
    g = -jnp.exp(alog_ref[...]) * _softplus(ar_ref[...] + dtb_ref[...])
    ti64 = lax.broadcasted_iota(jnp.int32, (c, c), 0)
    tj64 = lax.broadcasted_iota(jnp.int32, (c, c), 1)
    tri = jnp.where(ti64 >= tj64, 1.0, 0.0).astype(BF16)
    gc = _dot_mask_lhs(tri, g)
    g_t = jnp.concatenate([g, jnp.zeros_like(g)], axis=0).T
    ut = lax.broadcasted_iota(jnp.int32, (2 * c, 2 * c), 0)
    uj = lax.broadcasted_iota(jnp.int32, (2 * c, 2 * c), 1) & (c - 1)
    upper2 = jnp.where((ut <= uj) & (ut < c), 1.0, 0.0).astype(BF16)
    gct[...] = _dot_mask_rhs(g_t, upper2)
    for h in range(GDN_V_HEADS):
        gcb[h] = jnp.broadcast_to(gc[:, h:h + 1], (c, LANES))
        btb[h] = jnp.broadcast_to(beta[:, h:h + 1], (c, LANES))

    def conv_silu(col0, width):
        cs = slice(col0, col0 + width)
        return _silu(_causal_conv(halo[:, cs], qkv_ref[:, cs], cw_ref, cs))

    for p in range(GDN_PAIRS):
        q = conv_silu(hd * p, hd)
        q = q * lax.rsqrt(jnp.sum(q * q, axis=-1, keepdims=True) + RMS_EPS) * (hd ** -0.5)
        k = conv_silu(GDN_KEY_DIM + hd * p, hd)
        k = k * lax.rsqrt(jnp.sum(k * k, axis=-1, keepdims=True) + RMS_EPS)
        qs[p] = q
        ks[p] = k
        vs[p] = conv_silu(2 * GDN_KEY_DIM + 2 * hd * p, 2 * hd)
    halo[...] = qkv_ref[c - SUBLANES:c, :]

    ti = lax.broadcasted_iota(jnp.int32, (c, LANES), 0)
    lane = lax.broadcasted_iota(jnp.int32, (c, LANES), 1)
    left = lane < c
    tj = lane & (c - 1)
    left_row = lax.broadcasted_iota(jnp.int32, (1, LANES), 1) < c
    eye = jnp.where(ti == tj, 1.0, 0.0)
    left2 = lax.broadcasted_iota(jnp.int32, (c, 2 * hd), 1) < hd
    left2s = lax.broadcasted_iota(jnp.int32, (hd, 2 * hd), 1) < hd
    zero_tile = jnp.zeros((c, hd), F32)

    def block_diag(x, msk):
        return jnp.concatenate([jnp.where(msk, x, 0.0), jnp.where(msk, 0.0, x)], axis=0)

    def gc_row(p):
        return jnp.where(left_row, gct[2 * p:2 * p + 1, :], gct[2 * p + 1:2 * p + 2, :])

    for g0 in range(0, GDN_PAIRS, GDN_GROUP):
        grp = range(g0, g0 + GDN_GROUP)
        kq, a_qk, w_pow, t_inv, wu, r1, r2, v_new = {}, {}, {}, {}, {}, {}, {}, {}
        for p in grp:
            k = ks[p]
            kq[p] = _dot_nt(_bf(jnp.concatenate([k, qs[p]], axis=0)),
                            _bf(jnp.concatenate([k, k], axis=0)))
        for p in grp:
            col = jnp.where(left, gcb[2 * p], gcb[2 * p + 1])
            decay = jnp.where(ti >= tj, jnp.exp(jnp.minimum(col - gc_row(p), 0.0)), 0.0)
            a_kk = (jnp.where(ti > tj, kq[p][:c] * decay, 0.0)
                    * jnp.where(left, btb[2 * p], btb[2 * p + 1]))
            a_qk[p] = kq[p][c:] * decay
            w_pow[p] = _dot(_bf(a_kk), _bf(block_diag(a_kk, left)))
            t_inv[p] = eye - a_kk
        for _ in range(4):
            for p in grp:
                r = _dot(_bf(jnp.concatenate([w_pow[p], t_inv[p]], axis=0)),
                         _bf(block_diag(w_pow[p], left)))
                w_pow[p] = r[:c]
                t_inv[p] = t_inv[p] + r[c:]
        for p in grp:
            t_inv[p] = t_inv[p] + _dot(_bf(t_inv[p]), _bf(block_diag(w_pow[p], left)))
        for p in grp:
            k = ks[p]
            bt0 = btb[2 * p]
            bt1 = btb[2 * p + 1]
            v_ss = vs[p]
            rhs = jnp.concatenate([
                jnp.concatenate([k * (bt0 * jnp.exp(gcb[2 * p])), zero_tile,
                                 v_ss[:, :hd] * bt0, zero_tile], axis=1),
                jnp.concatenate([zero_tile, k * (bt1 * jnp.exp(gcb[2 * p + 1])),
                                 zero_tile, v_ss[:, hd:] * bt1], axis=1)], axis=0)
            wu[p] = _dot(_bf(t_inv[p]), _bf(rhs))
        for p in grp:
            q = qs[p]
            q_dec = jnp.concatenate([q * jnp.exp(gcb[2 * p]), q * jnp.exp(gcb[2 * p + 1])], axis=1)
            r1[p] = _dot(_bf(jnp.concatenate([wu[p][:, :2 * hd], q_dec], axis=0)),
                         _bf(block_diag(s_scr[p], left2s)))
        for p in grp:
            v_new[p] = wu[p][:, 2 * hd:] - r1[p][:c]
            k = ks[p]
            gl = jnp.where(left_row, gcb[2 * p][c - 1:c, :], gcb[2 * p + 1][c - 1:c, :])
            k_tail = jnp.concatenate([k, k], axis=0).T * jnp.exp(gl - gc_row(p))
            r2[p] = _dot(_bf(jnp.concatenate([a_qk[p], k_tail], axis=0)),
                         _bf(block_diag(v_new[p], left2)))
        for p in grp:
            o_scr[p] = r1[p][c:] + r2[p][:c]
            cd = jnp.concatenate([jnp.exp(gcb[2 * p][c - 1:c, :]),
                                  jnp.exp(gcb[2 * p + 1][c - 1:c, :])], axis=1)
            s_scr[p] = s_scr[p] * cd + r2[p][c:]

    nw = nw_ref[...]
    for p in range(GDN_PAIRS):
        o_ss = o_scr[p]
        for s in range(2):
            cs = slice(2 * hd * p + hd * s, 2 * hd * p + hd * (s + 1))
            o = o_ss[:, hd * s:hd * (s + 1)]
            nrm = o * lax.rsqrt(jnp.mean(o * o, axis=-1, keepdims=True) + RMS_EPS)
            o_ref[:, cs] = _bf(nrm * nw * _silu(z_ref[:, cs]))


def _gdn_core_call(u, u_gate, conv_w, a_log, dt_bias, norm_w, batch, seq, layer, w_ff1, w_ff2):
    n = u.shape[0]
    c = CHUNK
    cpb = seq // c
    hd = GDN_HEAD_DIM
    row = lambda b, t: b * cpb + t
    const2 = lambda b, t: (0, 0)
    job = _CastJob(layer, w_ff1, w_ff2, batch * cpb, row)
    return pl.pallas_call(
        _gdn_kernel,
        grid=(batch, cpb),
        in_specs=[
            pl.BlockSpec((c, GDN_CONV_DIM), lambda b, t: (row(b, t), 0)),
            pl.BlockSpec((c, GDN_VALUE_DIM), lambda b, t: (row(b, t), GDN_CONV_DIM // GDN_VALUE_DIM)),
            pl.BlockSpec((c, LANES), lambda b, t: (row(b, t), 0)),
            pl.BlockSpec((c, LANES), lambda b, t: (row(b, t), 1)),
            pl.BlockSpec((CONV_WIDTH, GDN_CONV_DIM), const2),
            pl.BlockSpec((1, LANES), const2),
            pl.BlockSpec((1, LANES), const2),
            pl.BlockSpec((1, hd), const2),
        ] + job.in_specs,
        out_specs=[pl.BlockSpec((c, GDN_VALUE_DIM), lambda b, t: (row(b, t), 0))] + job.out_specs,
        out_shape=[jax.ShapeDtypeStruct((n, GDN_VALUE_DIM), BF16)] + job.out_shape,
        scratch_shapes=[
            pltpu.VMEM((SUBLANES, GDN_CONV_DIM), F32),
            pltpu.VMEM((GDN_PAIRS, c, hd), F32),
            pltpu.VMEM((GDN_PAIRS, c, hd), F32),
            pltpu.VMEM((GDN_PAIRS, c, 2 * hd), F32),
            pltpu.VMEM((GDN_V_HEADS, c, LANES), F32),
            pltpu.VMEM((GDN_V_HEADS, c, LANES), F32),
            pltpu.VMEM((2 * c, 2 * c), F32),
            pltpu.VMEM((GDN_PAIRS, hd, 2 * hd), F32),
            pltpu.VMEM((GDN_PAIRS, c, 2 * hd), F32),
        ],
        compiler_params=_params("arbitrary", "arbitrary"),
        name="gdn_core",
    )(u, u, u_gate, u_gate, conv_w, a_log, dt_bias, norm_w, *job.args)


def _gla_kernel(q_ref, k_ref, v_ref, z_ref, al_ref, wup_ref, bal_ref, nw_ref, f1_ref, f2_ref,
                o_ref, f1b_ref, f2b_ref, s_scr, bc_scr):
    c = CHUNK
    dk = GLA_HEAD_K
    dv = GLA_HEAD_V
    _CastJob.run(f1_ref, f2_ref, f1b_ref, f2b_ref)

    nb = q_ref.shape[0]

    @pl.when(pl.program_id(0) == 0)
    def _():
        s_scr[...] = jnp.zeros_like(s_scr)

    ti = lax.broadcasted_iota(jnp.int32, (c, c), 0)
    tj = lax.broadcasted_iota(jnp.int32, (c, c), 1)
    causal = ti >= tj
    tri = jnp.where(causal, 1.0, 0.0).astype(BF16)
    nw = nw_ref[...]
    chains = [(b, h) for b in range(nb) for h in range(GLA_HEADS)]
    pre = {}
    for b in range(nb):
        pre[b] = _dot(_bf(al_ref[b]), wup_ref[...]) + bal_ref[...]
    for b in range(nb):
        log_alpha = (-1.0 / GLA_GATE_NORMALIZER) * _softplus(-pre[b])
        bc_scr[b] = _dot_mask_lhs(tri, log_alpha)
    q_dec, a_qk, o = {}, {}, {}
    for b, h in chains:
        ks_ = slice(dk * h, dk * (h + 1))
        bcum = bc_scr[b, :, ks_]
        q_dec[b, h] = _bf(q_ref[b, :, ks_] * (dk ** -0.5) * jnp.exp(bcum))
        a_qk[b, h] = _dot_nt(q_dec[b, h], _bf(k_ref[b, :, ks_] * jnp.exp(-bcum)))
    for b, h in chains:
        v = _bf(v_ref[b, :, dv * h:dv * (h + 1)])
        s_t = s_scr[b * GLA_HEADS + h]
        o[b, h] = (_dot_nt(q_dec[b, h], _bf(s_t))
                   + _dot(_bf(jnp.where(causal, a_qk[b, h], 0.0)), v))
    for b, h in chains:
        ks_ = slice(dk * h, dk * (h + 1))
        bcum = bc_scr[b, :, ks_]
        b_last = bcum[c - 1:c, :]
        k_tail = _bf(k_ref[b, :, ks_] * jnp.exp(b_last - bcum))
        i = b * GLA_HEADS + h
        s_scr[i] = s_scr[i] * jnp.exp(b_last) + _dot_tn(_bf(v_ref[b, :, dv * h:dv * (h + 1)]), k_tail)
    for b, h in chains:
        vs_ = slice(dv * h, dv * (h + 1))
        oh = o[b, h]
        nrm = oh * lax.rsqrt(jnp.mean(oh * oh, axis=-1, keepdims=True) + RMS_EPS)
        o_ref[b, :, vs_] = _bf(nrm * nw * _silu(z_ref[b, :, vs_]))


def _gla_core_call(u, u_gate, w_up, b_alpha, norm_w, batch, seq, layer, w_ff1, w_ff2):
    c = CHUNK
    u3 = u.reshape(batch, seq, u.shape[1])
    g3 = u_gate.reshape(batch, seq, u_gate.shape[1])
    const2 = lambda t: (0, 0)
    job = _CastJob(layer, w_ff1, w_ff2, seq // c, lambda t: t)
    out, w1b, w2b = pl.pallas_call(
        _gla_kernel,
        grid=(seq // c,),
        in_specs=[
            pl.BlockSpec((batch, c, GLA_KEY_DIM), lambda t: (0, t, 0)),
            pl.BlockSpec((batch, c, GLA_KEY_DIM), lambda t: (0, t, 1)),
            pl.BlockSpec((batch, c, GLA_VALUE_DIM), lambda t: (0, t, 1)),
            pl.BlockSpec((batch, c, GLA_VALUE_DIM), lambda t: (0, t, 2)),
            pl.BlockSpec((batch, c, LANES), lambda t: (0, t, 0)),
            pl.BlockSpec((LANES, GLA_KEY_DIM), const2),
            pl.BlockSpec((1, GLA_KEY_DIM), const2),
            pl.BlockSpec((1, GLA_HEAD_V), const2),
        ] + job.in_specs,
        out_specs=[pl.BlockSpec((batch, c, GLA_VALUE_DIM), lambda t: (0, t, 0))] + job.out_specs,
        out_shape=[jax.ShapeDtypeStruct((batch, seq, GLA_VALUE_DIM), BF16)] + job.out_shape,
        scratch_shapes=[pltpu.VMEM((batch * GLA_HEADS, GLA_HEAD_V, GLA_HEAD_K), F32),
                        pltpu.VMEM((batch, c, GLA_KEY_DIM), F32)],
        compiler_params=_params("arbitrary"),
        name="gla_core",
    )(u3, u3, u3, u3, g3, w_up, b_alpha, norm_w, *job.args)
    return out.reshape(batch * seq, GLA_VALUE_DIM), w1b, w2b


def _pad_cols(w, total):
    return jnp.pad(w, ((0, 0), (0, total - w.shape[1])))


def _rglru_gate_weights(w_rgate, b_rgate, w_igate, b_igate):
    per = RNN_GROUP // RNN_BLOCK_DIM
    eye = jnp.eye(per, dtype=F32)

    def dense(w):
        w = w.reshape(RNN_GROUPS, per, RNN_BLOCK_DIM, RNN_BLOCK_DIM)
        return jnp.einsum("gjde,jk->gjdke", w, eye).reshape(RNN_GROUPS, RNN_GROUP, RNN_GROUP)

    wg = jnp.concatenate([dense(w_rgate), dense(w_igate)], axis=-1).astype(BF16)
    bg = jnp.concatenate([b_rgate.reshape(RNN_GROUPS, 1, RNN_GROUP),
                          b_igate.reshape(RNN_GROUPS, 1, RNN_GROUP)], axis=-1)
    return wg, bg


def _gdn_gate_weights(w_in):
    main = GDN_CONV_DIM + GDN_VALUE_DIM
    b_w = _pad_cols(w_in[:, main:main + GDN_V_HEADS], LANES)
    a_w = _pad_cols(w_in[:, main + GDN_V_HEADS:main + 2 * GDN_V_HEADS], LANES)
    return jnp.concatenate([b_w, a_w], axis=1)


def _gla_gate_weights(w_in):
    main = 2 * GLA_KEY_DIM + 2 * GLA_VALUE_DIM
    return _pad_cols(w_in[:, main:main + GLA_GATE_RANK], LANES)


def kernel(x, c, ln_g, ln_b, w_mod, b_mod, w_ff1, w_ff2, rglru_w_in, rglru_conv_w, rglru_conv_b, rglru_w_rgate, rglru_b_rgate, rglru_w_igate, rglru_b_igate, rglru_lambda, rglru_w_out, gdn_w_in, gdn_conv_w, gdn_a_log, gdn_dt_bias, gdn_norm_w, gdn_w_out, gla_w_in, gla_w_alpha_up, gla_b_alpha, gla_norm_w, gla_w_out):
    batch, seq, d = x.shape
    depth = w_mod.shape[0]
    n = batch * seq
    xf = x.reshape(n, d)

    assert batch <= MOD_ROWS
    c_pad = jnp.pad(c, ((0, MOD_ROWS - batch), (0, 0)))
    modv = _mod_call(c_pad, w_mod, b_mod).reshape(depth * MOD_ROWS * N_MOD, 1, d)
    lng = ln_g.reshape(depth * 2, 1, d)
    lnb = ln_b.reshape(depth * 2, 1, d)
    rglru_w_in_b = rglru_w_in.astype(BF16)
    gdn_w_in_b = gdn_w_in.astype(BF16)
    gla_w_in_b = gla_w_in.astype(BF16)
    rglru_w_out_b = rglru_w_out.astype(BF16)
    gdn_w_out_b = gdn_w_out.astype(BF16)
    gla_w_out_b = gla_w_out.astype(BF16)
    gdn_main = GDN_CONV_DIM + GDN_VALUE_DIM
    gla_main = 2 * GLA_KEY_DIM + 2 * GLA_VALUE_DIM

    for i in range(depth):
        kind, slot = i % N_MIXERS, i // N_MIXERS
        if kind == 0:
            u, _ = _proj_call(xf, modv, i, rglru_w_in_b, slot, 2 * RNN_WIDTH, seq)
            wg, bg = _rglru_gate_weights(rglru_w_rgate[slot], rglru_b_rgate[slot],
                                         rglru_w_igate[slot], rglru_b_igate[slot])
            y, w1b, w2b = _rglru_core_call(u, rglru_conv_w[slot], rglru_conv_b[slot][None], wg, bg,
                                           rglru_lambda[slot][None], batch, seq, i, w_ff1, w_ff2)
            w_out = rglru_w_out_b
        elif kind == 1:
            u, u_gate = _proj_call(xf, modv, i, gdn_w_in_b, slot, gdn_main, seq,
                                   w_small=_gdn_gate_weights(gdn_w_in_b[slot]))
            y, w1b, w2b = _gdn_core_call(u, u_gate, gdn_conv_w[slot],
                                         _pad_cols(gdn_a_log[slot][None], LANES),
                                         _pad_cols(gdn_dt_bias[slot][None], LANES),
                                         gdn_norm_w[slot][None], batch, seq, i, w_ff1, w_ff2)
            w_out = gdn_w_out_b
        else:
            u, u_gate = _proj_call(xf, modv, i, gla_w_in_b, slot, gla_main, seq,
                                   w_small=_gla_gate_weights(gla_w_in_b[slot]))
            w_up = jnp.pad(gla_w_alpha_up[slot], ((0, LANES - GLA_GATE_RANK), (0, 0))).astype(BF16)
            y, w1b, w2b = _gla_core_call(u, u_gate, w_up, gla_b_alpha[slot][None], gla_norm_w[slot][None],
                                         batch, seq, i, w_ff1, w_ff2)
            w_out = gla_w_out_b
        xf = _outln_call(y, w_out, slot, xf, modv, lng, lnb, i, seq)
        xf = _mlp_call(xf, modv, w1b, w2b, lng, lnb, i, seq)
    return xf.reshape(batch, seq, d)
```

```python
import functools

import jax
import jax.numpy as jnp
from jax import lax
from jax.experimental import pallas as pl
from jax.experimental.pallas import tpu as pltpu

F32 = jnp.float32
BF16 = jnp.bfloat16

D_MODEL = 2048
DEPTH = 4
N_MIXERS = 3
DEEPNORM_ALPHA = float((2 * DEPTH) ** 0.25)
LN_EPS = 1e-5
RMS_EPS = 1e-6
D_FF = 4 * D_MODEL
N_MOD = 6
MOD_ROWS = 8
CONV_WIDTH = 4

RNN_WIDTH = (5 * D_MODEL) // 4
RNN_BLOCKS = 16
RNN_BLOCK_DIM = RNN_WIDTH // RNN_BLOCKS
RNN_GROUP = 640
RNN_GROUPS = RNN_WIDTH // RNN_GROUP
LRU_C = 8.0

GDN_HEAD_DIM = 128
GDN_QK_HEADS = D_MODEL // GDN_HEAD_DIM
GDN_V_HEADS = 2 * GDN_QK_HEADS
GDN_KEY_DIM = GDN_QK_HEADS * GDN_HEAD_DIM
GDN_VALUE_DIM = GDN_V_HEADS * GDN_HEAD_DIM
GDN_CONV_DIM = 2 * GDN_KEY_DIM + GDN_VALUE_DIM
GDN_PAIRS = GDN_QK_HEADS
GDN_GROUP = 16

GLA_HEADS = 4
GLA_KEY_DIM = D_MODEL // 2
GLA_VALUE_DIM = D_MODEL
GLA_HEAD_K = GLA_KEY_DIM // GLA_HEADS
GLA_HEAD_V = GLA_VALUE_DIM // GLA_HEADS
GLA_GATE_RANK = 16
GLA_GATE_NORMALIZER = 16.0

CHUNK = 64
LANES = 128
SUBLANES = 8
OUTLN_SUBTILES = 4
VMEM_LIMIT = 56 * 1024 * 1024


def _dot(a, b):
    return jnp.dot(a, b, preferred_element_type=F32)


def _dot_nt(a, b):
    return lax.dot_general(a, b, (((1,), (1,)), ((), ())), preferred_element_type=F32)


def _dot_tn(a, b):
    return lax.dot_general(a, b, (((0,), (0,)), ((), ())), preferred_element_type=F32)


def _bf(x):
    return x.astype(BF16)


def _split3(x):
    hi = _bf(x)
    r1 = x - hi.astype(F32)
    mid = _bf(r1)
    lo = _bf(r1 - mid.astype(F32))
    return hi, mid, lo


def _dot_mask_lhs(m, x):
    hi, mid, lo = _split3(x)
    return _dot(m, hi) + _dot(m, mid) + _dot(m, lo)


def _dot_mask_rhs(x, m):
    hi, mid, lo = _split3(x)
    return _dot(hi, m) + _dot(mid, m) + _dot(lo, m)


def _softplus(x):
    return jnp.maximum(x, 0.0) + jnp.log1p(jnp.exp(-jnp.abs(x)))


def _sigmoid(x):
    return 0.5 * jnp.tanh(0.5 * x) + 0.5


def _silu(x):
    return x * _sigmoid(x)


def _params(*sem):
    return pltpu.CompilerParams(dimension_semantics=sem, vmem_limit_bytes=VMEM_LIMIT)


def _causal_conv(prev, x, cw_ref, cs):
    x_ext = jnp.concatenate([prev, x], axis=0)
    acc = cw_ref[CONV_WIDTH - 1:CONV_WIDTH, cs] * x
    for s in range(1, CONV_WIDTH):
        shifted = pltpu.roll(x_ext, s, 0)[SUBLANES:]
        acc = acc + cw_ref[CONV_WIDTH - 1 - s:CONV_WIDTH - s, cs] * shifted
    return acc


def _layer_norm_rows(r, g, b):
    mu = jnp.mean(r, axis=-1, keepdims=True)
    d = r - mu
    var = jnp.mean(d * d, axis=-1, keepdims=True)
    return d * lax.rsqrt(var + LN_EPS) * g + b


def _mod_kernel(c_ref, w_ref, b_ref, o_ref):
    c = c_ref[...]
    o_ref[0] = _dot(_bf(_silu(c)), _bf(w_ref[0])) + b_ref[0]


def _mod_call(c_pad, w_mod, b_mod):
    depth, d, n = w_mod.shape
    rows = c_pad.shape[0]
    tn = 1024
    return pl.pallas_call(
        _mod_kernel,
        grid=(depth, n // tn),
        in_specs=[
            pl.BlockSpec((rows, d), lambda l, j: (0, 0)),
            pl.BlockSpec((1, d, tn), lambda l, j: (l, 0, j)),
            pl.BlockSpec((1, 1, tn), lambda l, j: (l, 0, j)),
        ],
        out_specs=pl.BlockSpec((1, rows, tn), lambda l, j: (l, 0, j)),
        out_shape=jax.ShapeDtypeStruct((depth, rows, n), F32),
        compiler_params=_params("arbitrary", "arbitrary"),
        name="mod",
    )(c_pad, w_mod, b_mod.reshape(depth, 1, n))


def _mod_spec(layer, which, tpb, d):
    return pl.BlockSpec((None, 1, d),
                        lambda i, *_: ((layer * MOD_ROWS + i // tpb) * N_MOD + which, 0, 0))


def _ln_spec(layer, which, d):
    return pl.BlockSpec((None, 1, d), lambda i, *_: (2 * layer + which, 0, 0))


def _proj_kernel(has_small, x_ref, sh_ref, sc_ref, w_ref, *rest):
    if has_small:
        ws_ref, o_ref, os_ref, h_scr = rest
    else:
        o_ref, h_scr = rest

    @pl.when(pl.program_id(1) == 0)
    def _():
        h = _bf(x_ref[...] * (1.0 + sc_ref[...]) + sh_ref[...])
        h_scr[...] = h
        if has_small:
            os_ref[...] = _dot(h, ws_ref[...])

    o_ref[...] = _dot(h_scr[...], w_ref[...])


def _proj_tn(nout):
    return next(t for t in (1280, 1024) if nout % t == 0)


def _proj_call(x, modv, layer, w, slot, nout, seq, w_small=None):
    n, d = x.shape
    tm = min(1024, seq)
    tn = _proj_tn(nout)
    tpb = seq // tm
    has_small = w_small is not None
    in_specs = [
        pl.BlockSpec((tm, d), lambda i, j: (i, 0)),
        _mod_spec(layer, 0, tpb, d),
        _mod_spec(layer, 1, tpb, d),
        pl.BlockSpec((None, d, tn), lambda i, j: (slot, 0, j)),
    ]
    out_specs = [pl.BlockSpec((tm, tn), lambda i, j: (i, j))]
    out_shape = [jax.ShapeDtypeStruct((n, nout), F32)]
    args = [x, modv, modv, w]
    if has_small:
        ns = w_small.shape[1]
        in_specs.append(pl.BlockSpec((d, ns), lambda i, j: (0, 0)))
        out_specs.append(pl.BlockSpec((tm, ns), lambda i, j: (i, 0)))
        out_shape.append(jax.ShapeDtypeStruct((n, ns), F32))
        args.append(w_small)
    outs = pl.pallas_call(
        functools.partial(_proj_kernel, has_small),
        grid=(n // tm, nout // tn),
        in_specs=in_specs,
        out_specs=out_specs,
        out_shape=out_shape,
        scratch_shapes=[pltpu.VMEM((tm, d), BF16)],
        compiler_params=_params("arbitrary", "arbitrary"),
        name="proj",
    )(*args)
    return tuple(outs) if has_small else (outs[0], None)


class _CastJob:
    def __init__(self, layer, w_ff1, w_ff2, steps, step_of):
        _, d, dff = w_ff1.shape
        nch = next(c for c in (128, 64, 32, 16, 8, 4, 2, 1) if c <= steps)
        r1, r2 = d // nch, dff // nch
        chunk = lambda *idx: jnp.minimum(step_of(*idx), nch - 1)
        self.args = [w_ff1, w_ff2]
        self.in_specs = [pl.BlockSpec((None, r1, dff), lambda *idx: (layer, chunk(*idx), 0)),
                         pl.BlockSpec((None, r2, d), lambda *idx: (layer, chunk(*idx), 0))]
        self.out_specs = [pl.BlockSpec((r1, dff), lambda *idx: (chunk(*idx), 0)),
                          pl.BlockSpec((r2, d), lambda *idx: (chunk(*idx), 0))]
        self.out_shape = [jax.ShapeDtypeStruct((d, dff), BF16), jax.ShapeDtypeStruct((dff, d), BF16)]

    @staticmethod
    def run(f1_ref, f2_ref, f1b_ref, f2b_ref):
        f1b_ref[...] = _bf(f1_ref[...])
        f2b_ref[...] = _bf(f2_ref[...])


def _outln_kernel(y_ref, w_ref, x_ref, gt_ref, g_ref, b_ref, o_ref):
    sub = y_ref.shape[0] // OUTLN_SUBTILES
    for s in range(OUTLN_SUBTILES):
        rs = slice(s * sub, (s + 1) * sub)
        r = DEEPNORM_ALPHA * x_ref[rs, :] + (1.0 + gt_ref[...]) * _dot(y_ref[rs, :], w_ref[...])
        o_ref[rs, :] = _layer_norm_rows(r, g_ref[...], b_ref[...])


def _outln_call(y, w, slot, x, modv, lng, lnb, layer, seq):
    n, d = x.shape
    kdim = y.shape[1]
    tm = min(512, seq)
    tpb = seq // tm
    return pl.pallas_call(
        _outln_kernel,
        grid=(n // tm,),
        in_specs=[
            pl.BlockSpec((tm, kdim), lambda i: (i, 0)),
            pl.BlockSpec((None, kdim, d), lambda i: (slot, 0, 0), pipeline_mode=pl.Buffered(1)),
            pl.BlockSpec((tm, d), lambda i: (i, 0)),
            _mod_spec(layer, 2, tpb, d),
            _ln_spec(layer, 0, d),
            _ln_spec(layer, 0, d),
        ],
        out_specs=pl.BlockSpec((tm, d), lambda i: (i, 0)),
        out_shape=jax.ShapeDtypeStruct((n, d), F32),
        compiler_params=_params("arbitrary"),
        name="outln",
    )(y, w, x, modv, lng, lnb)


def _mlp_kernel(x_ref, sh_ref, sc_ref, gt_ref, w1_ref, w2_ref, g_ref, b_ref, o_ref, h_scr):
    k = pl.program_id(1)

    @pl.when(k == 0)
    def _():
        h_scr[...] = _bf(x_ref[...] * (1.0 + sc_ref[...]) + sh_ref[...])
        o_ref[...] = jnp.zeros_like(o_ref)

    hid = jnp.maximum(_dot(h_scr[...], w1_ref[...]), 0.0)
    o_ref[...] += _dot(_bf(hid * hid), w2_ref[...])

    @pl.when(k == pl.num_programs(1) - 1)
    def _():
        r = DEEPNORM_ALPHA * x_ref[...] + (1.0 + gt_ref[...]) * o_ref[...]
        o_ref[...] = _layer_norm_rows(r, g_ref[...], b_ref[...])


def _mlp_call(x, modv, w1, w2, lng, lnb, layer, seq):
    n, d = x.shape
    dff = w1.shape[1]
    tm = min(512, seq)
    tf = 1024
    tpb = seq // tm
    return pl.pallas_call(
        _mlp_kernel,
        grid=(n // tm, dff // tf),
        in_specs=[
            pl.BlockSpec((tm, d), lambda i, k: (i, 0)),
            _mod_spec(layer, 3, tpb, d),
            _mod_spec(layer, 4, tpb, d),
            _mod_spec(layer, 5, tpb, d),
            pl.BlockSpec((d, tf), lambda i, k: (0, k)),
            pl.BlockSpec((tf, d), lambda i, k: (k, 0)),
            _ln_spec(layer, 1, d),
            _ln_spec(layer, 1, d),
        ],
        out_specs=pl.BlockSpec((tm, d), lambda i, k: (i, 0)),
        out_shape=jax.ShapeDtypeStruct((n, d), F32),
        scratch_shapes=[pltpu.VMEM((tm, d), BF16)],
        compiler_params=_params("arbitrary", "arbitrary"),
        name="mlp",
    )(x, modv, modv, modv, w1, w2, lng, lnb)


def _rglru_kernel(gate_ref, rec_ref, cw_ref, cb_ref, wg_ref, bg_ref, lam_ref, f1_ref, f2_ref,
                  o_ref, f1b_ref, f2b_ref, halo, abuf, bbuf, hcar):
    tm = rec_ref.shape[0]
    _CastJob.run(f1_ref, f2_ref, f1b_ref, f2b_ref)

    @pl.when(pl.program_id(1) == 0)
    def _():
        halo[...] = jnp.zeros_like(halo)
        hcar[...] = jnp.zeros_like(hcar)

    sp = _softplus(-lam_ref[...])
    for g in range(RNN_GROUPS):
        cs = slice(RNN_GROUP * g, RNN_GROUP * (g + 1))
        xr = _causal_conv(halo[:, cs], rec_ref[:, cs], cw_ref, cs) + cb_ref[:, cs]
        pre = _dot(_bf(xr), wg_ref[g]) + bg_ref[g]
        r_gate = _sigmoid(pre[:, :RNN_GROUP])
        i_gate = _sigmoid(pre[:, RNN_GROUP:])
        neg_log_a = LRU_C * r_gate * sp[:, cs]
        a = jnp.exp(-neg_log_a)
        abuf[:, cs] = a
        bbuf[:, cs] = jnp.sqrt(jnp.tanh(neg_log_a) * (1.0 + a * a)) * (i_gate * xr)
    halo[...] = rec_ref[tm - SUBLANES:tm, :]

    row = lax.broadcasted_iota(jnp.int32, (SUBLANES, RNN_WIDTH), 0)

    def body(j, carry):
        r0 = pl.multiple_of(j * SUBLANES, SUBLANES)
        a = abuf[pl.ds(r0, SUBLANES), :]
        b = bbuf[pl.ds(r0, SUBLANES), :]
        for k in (1, 2, 4):
            a_s = jnp.where(row >= k, pltpu.roll(a, k, 0), 1.0)
            b_s = jnp.where(row >= k, pltpu.roll(b, k, 0), 0.0)
            b = a * b_s + b
            a = a * a_s
        h = a * carry + b
        bbuf[pl.ds(r0, SUBLANES), :] = h
        return h[SUBLANES - 1:SUBLANES, :]

    hcar[...] = lax.fori_loop(0, tm // SUBLANES, body, hcar[...])
    o_ref[...] = _bf(jax.nn.gelu(gate_ref[...], approximate=True) * bbuf[...])


def _rglru_core_call(u, conv_w, conv_b, wg, bg, lam, batch, seq, layer, w_ff1, w_ff2):
    n = u.shape[0]
    tm = min(256, seq)
    tpb = seq // tm
    r = RNN_WIDTH
    const2 = lambda b, t: (0, 0)
    job = _CastJob(layer, w_ff1, w_ff2, batch * tpb, lambda b, t: b * tpb + t)
    return pl.pallas_call(
        _rglru_kernel,
        grid=(batch, tpb),
        in_specs=[
            pl.BlockSpec((tm, r), lambda b, t: (b * tpb + t, 0)),
            pl.BlockSpec((tm, r), lambda b, t: (b * tpb + t, 1)),
            pl.BlockSpec((CONV_WIDTH, r), const2),
            pl.BlockSpec((1, r), const2),
            pl.BlockSpec((RNN_GROUPS, RNN_GROUP, 2 * RNN_GROUP), lambda b, t: (0, 0, 0)),
            pl.BlockSpec((RNN_GROUPS, 1, 2 * RNN_GROUP), lambda b, t: (0, 0, 0)),
            pl.BlockSpec((1, r), const2),
        ] + job.in_specs,
        out_specs=[pl.BlockSpec((tm, r), lambda b, t: (b * tpb + t, 0))] + job.out_specs,
        out_shape=[jax.ShapeDtypeStruct((n, r), BF16)] + job.out_shape,
        scratch_shapes=[
            pltpu.VMEM((SUBLANES, r), F32),
            pltpu.VMEM((tm, r), F32),
            pltpu.VMEM((tm, r), F32),
            pltpu.VMEM((1, r), F32),
        ],
        compiler_params=_params("arbitrary", "arbitrary"),
        name="rglru_core",
    )(u, u, conv_w, conv_b, wg, bg, lam, *job.args)


def _gdn_kernel(qkv_ref, z_ref, br_ref, ar_ref, cw_ref, alog_ref, dtb_ref, nw_ref, f1_ref, f2_ref,
                o_ref, f1b_ref, f2b_ref, halo, qs, ks, vs, gcb, btb, gct, s_scr, o_scr):
    c = CHUNK
    hd = GDN_HEAD_DIM
    _CastJob.run(f1_ref, f2_ref, f1b_ref, f2b_ref)

    @pl.when(pl.program_id(1) == 0)
    def _():
        halo[...] = jnp.zeros_like(halo)
        s_scr[...] = jnp.zeros_like(s_scr)

    beta = _sigmoid(br_ref[...])
    g = -jnp.exp(alog_ref[...]) * _softplus(ar_ref[...] + dtb_ref[...])
    ti64 = lax.broadcasted_iota(jnp.int32, (c, c), 0)
    tj64 = lax.broadcasted_iota(jnp.int32, (c, c), 1)
    tri = jnp.where(ti64 >= tj64, 1.0, 0.0).astype(BF16)
    gc = _dot_mask_lhs(tri, g)
    g_t = jnp.concatenate([g, jnp.zeros_like(g)], axis=0).T
    ut = lax.broadcasted_iota(jnp.int32, (2 * c, 2 * c), 0)
    uj = lax.broadcasted_iota(jnp.int32, (2 * c, 2 * c), 1) & (c - 1)
    upper2 = jnp.where((ut <= uj) & (ut < c), 1.0, 0.0).astype(BF16)
    gct[...] = _dot_mask_rhs(g_t, upper2)
    for h in range(GDN_V_HEADS):
        gcb[h] = jnp.broadcast_to(gc[:, h:h + 1], (c, LANES))
        btb[h] = jnp.broadcast_to(beta[:, h:h + 1], (c, LANES))

    def conv_silu(col0, width):
        cs = slice(col0, col0 + width)
        return _silu(_causal_conv(halo[:, cs], qkv_ref[:, cs], cw_ref, cs))

    for p in range(GDN_PAIRS):
        q = conv_silu(hd * p, hd)
        q = q * lax.rsqrt(jnp.sum(q * q, axis=-1, keepdims=True) + RMS_EPS) * (hd ** -0.5)
        k = conv_silu(GDN_KEY_DIM + hd * p, hd)
        k = k * lax.rsqrt(jnp.sum(k * k, axis=-1, keepdims=True) + RMS_EPS)
        qs[p] = q
        ks[p] = k
        vs[p] = conv_silu(2 * GDN_KEY_DIM + 2 * hd * p, 2 * hd)
    halo[...] = qkv_ref[c - SUBLANES:c, :]

    ti = lax.broadcasted_iota(jnp.int32, (c, LANES), 0)
    lane = lax.broadcasted_iota(jnp.int32, (c, LANES), 1)
    left = lane < c
    tj = lane & (c - 1)
    left_row = lax.broadcasted_iota(jnp.int32, (1, LANES), 1) < c
    eye = jnp.where(ti == tj, 1.0, 0.0)
    left2 = lax.broadcasted_iota(jnp.int32, (c, 2 * hd), 1) < hd
    left2s = lax.broadcasted_iota(jnp.int32, (hd, 2 * hd), 1) < hd
    zero_tile = jnp.zeros((c, hd), F32)

    def block_diag(x, msk):
        return jnp.concatenate([jnp.where(msk, x, 0.0), jnp.where(msk, 0.0, x)], axis=0)

    def gc_row(p):
        return jnp.where(left_row, gct[2 * p:2 * p + 1, :], gct[2 * p + 1:2 * p + 2, :])

    for g0 in range(0, GDN_PAIRS, GDN_GROUP):
        grp = range(g0, g0 + GDN_GROUP)
        kq, a_qk, w_pow, t_inv, wu, r1, r2, v_new = {}, {}, {}, {}, {}, {}, {}, {}
        for p in grp:
            k = ks[p]
            kq[p] = _dot_nt(_bf(jnp.concatenate([k, qs[p]], axis=0)),
                            _bf(jnp.concatenate([k, k], axis=0)))
        for p in grp:
            col = jnp.where(left, gcb[2 * p], gcb[2 * p + 1])
            decay = jnp.where(ti >= tj, jnp.exp(jnp.minimum(col - gc_row(p), 0.0)), 0.0)
            a_kk = (jnp.where(ti > tj, kq[p][:c] * decay, 0.0)
                    * jnp.where(left, btb[2 * p], btb[2 * p + 1]))
            a_qk[p] = kq[p][c:] * decay
            w_pow[p] = _dot(_bf(a_kk), _bf(block_diag(a_kk, left)))
            t_inv[p] = eye - a_kk
        for _ in range(4):
            for p in grp:
                r = _dot(_bf(jnp.concatenate([w_pow[p], t_inv[p]], axis=0)),
                         _bf(block_diag(w_pow[p], left)))
                w_pow[p] = r[:c]
                t_inv[p] = t_inv[p] + r[c:]
        for p in grp:
            t_inv[p] = t_inv[p] + _dot(_bf(t_inv[p]), _bf(block_diag(w_pow[p], left)))
        for p in grp:
            k = ks[p]
            bt0 = btb[2 * p]
            bt1 = btb[2 * p + 1]
            v_ss = vs[p]
            rhs = jnp.concatenate([
                jnp.concatenate([k * (bt0 * jnp.exp(gcb[2 * p])), zero_tile,
                                 v_ss[:, :hd] * bt0, zero_tile], axis=1),
                jnp.concatenate([zero_tile, k * (bt1 * jnp.exp(gcb[2 * p + 1])),
                                 zero_tile, v_ss[:, hd:] * bt1], axis=1)], axis=0)
            wu[p] = _dot(_bf(t_inv[p]), _bf(rhs))
        for p in grp:
            q = qs[p]
            q_dec = jnp.concatenate([q * jnp.exp(gcb[2 * p]), q * jnp.exp(gcb[2 * p + 1])], axis=1)
            r1[p] = _dot(_bf(jnp.concatenate([wu[p][:, :2 * hd], q_dec], axis=0)),
                         _bf(block_diag(s_scr[p], left2s)))
        for p in grp:
            v_new[p] = wu[p][:, 2 * hd:] - r1[p][:c]
            k = ks[p]
            gl = jnp.where(left_row, gcb[2 * p][c - 1:c, :], gcb[2 * p + 1][c - 1:c, :])
            k_tail = jnp.concatenate([k, k], axis=0).T * jnp.exp(gl - gc_row(p))
            r2[p] = _dot(_bf(jnp.concatenate([a_qk[p], k_tail], axis=0)),
                         _bf(block_diag(v_new[p], left2)))
        for p in grp:
            o_scr[p] = r1[p][c:] + r2[p][:c]
            cd = jnp.concatenate([jnp.exp(gcb[2 * p][c - 1:c, :]),
                                  jnp.exp(gcb[2 * p + 1][c - 1:c, :])], axis=1)
            s_scr[p] = s_scr[p] * cd + r2[p][c:]

    nw = nw_ref[...]
    for p in range(GDN_PAIRS):
        o_ss = o_scr[p]
        for s in range(2):
            cs = slice(2 * hd * p + hd * s, 2 * hd * p + hd * (s + 1))
            o = o_ss[:, hd * s:hd * (s + 1)]
            nrm = o * lax.rsqrt(jnp.mean(o * o, axis=-1, keepdims=True) + RMS_EPS)
            o_ref[:, cs] = _bf(nrm * nw * _silu(z_ref[:, cs]))


def _gdn_core_call(u, u_gate, conv_w, a_log, dt_bias, norm_w, batch, seq, layer, w_ff1, w_ff2):
    n = u.shape[0]
    c = CHUNK
    cpb = seq // c
    hd = GDN_HEAD_DIM
    row = lambda b, t: b * cpb + t
    const2 = lambda b, t: (0, 0)
    job = _CastJob(layer, w_ff1, w_ff2, batch * cpb, row)
    return pl.pallas_call(
        _gdn_kernel,
        grid=(batch, cpb),
        in_specs=[
            pl.BlockSpec((c, GDN_CONV_DIM), lambda b, t: (row(b, t), 0)),
            pl.BlockSpec((c, GDN_VALUE_DIM), lambda b, t: (row(b, t), GDN_CONV_DIM // GDN_VALUE_DIM)),
            pl.BlockSpec((c, LANES), lambda b, t: (row(b, t), 0)),
            pl.BlockSpec((c, LANES), lambda b, t: (row(b, t), 1)),
            pl.BlockSpec((CONV_WIDTH, GDN_CONV_DIM), const2),
            pl.BlockSpec((1, LANES), const2),
            pl.BlockSpec((1, LANES), const2),
            pl.BlockSpec((1, hd), const2),
        ] + job.in_specs,
        out_specs=[pl.BlockSpec((c, GDN_VALUE_DIM), lambda b, t: (row(b, t), 0))] + job.out_specs,
        out_shape=[jax.ShapeDtypeStruct((n, GDN_VALUE_DIM), BF16)] + job.out_shape,
        scratch_shapes=[
            pltpu.VMEM((SUBLANES, GDN_CONV_DIM), F32),
            pltpu.VMEM((GDN_PAIRS, c, hd), F32),
            pltpu.VMEM((GDN_PAIRS, c, hd), F32),
            pltpu.VMEM((GDN_PAIRS, c, 2 * hd), F32),
            pltpu.VMEM((GDN_V_HEADS, c, LANES), F32),
            pltpu.VMEM((GDN_V_HEADS, c, LANES), F32),
            pltpu.VMEM((2 * c, 2 * c), F32),
            pltpu.VMEM((GDN_PAIRS, hd, 2 * hd), F32),
            pltpu.VMEM((GDN_PAIRS, c, 2 * hd), F32),
        ],
        compiler_params=_params("arbitrary", "arbitrary"),
        name="gdn_core",
    )(u, u, u_gate, u_gate, conv_w, a_log, dt_bias, norm_w, *job.args)


def _gla_kernel(q_ref, k_ref, v_ref, z_ref, al_ref, wup_ref, bal_ref, nw_ref, f1_ref, f2_ref,
                o_ref, f1b_ref, f2b_ref, s_scr, bc_scr):
    c = CHUNK
    dk = GLA_HEAD_K
    dv = GLA_HEAD_V
    _CastJob.run(f1_ref, f2_ref, f1b_ref, f2b_ref)

    nb = q_ref.shape[0]

    @pl.when(pl.program_id(0) == 0)
    def _():
        s_scr[...] = jnp.zeros_like(s_scr)

    ti = lax.broadcasted_iota(jnp.int32, (c, c), 0)
    tj = lax.broadcasted_iota(jnp.int32, (c, c), 1)
    causal = ti >= tj
    tri = jnp.where(causal, 1.0, 0.0).astype(BF16)
    nw = nw_ref[...]
    chains = [(b, h) for b in range(nb) for h in range(GLA_HEADS)]
    pre = {}
    for b in range(nb):
        pre[b] = _dot(_bf(al_ref[b]), wup_ref[...]) + bal_ref[...]
    for b in range(nb):
        log_alpha = (-1.0 / GLA_GATE_NORMALIZER) * _softplus(-pre[b])
        bc_scr[b] = _dot_mask_lhs(tri, log_alpha)
    q_dec, a_qk, o = {}, {}, {}
    for b, h in chains:
        ks_ = slice(dk * h, dk * (h + 1))
        bcum = bc_scr[b, :, ks_]
        q_dec[b, h] = _bf(q_ref[b, :, ks_] * (dk ** -0.5) * jnp.exp(bcum))
        a_qk[b, h] = _dot_nt(q_dec[b, h], _bf(k_ref[b, :, ks_] * jnp.exp(-bcum)))
    for b, h in chains:
        v = _bf(v_ref[b, :, dv * h:dv * (h + 1)])
        s_t = s_scr[b * GLA_HEADS + h]
        o[b, h] = (_dot_nt(q_dec[b, h], _bf(s_t))
                   + _dot(_bf(jnp.where(causal, a_qk[b, h], 0.0)), v))
    for b, h in chains:
        ks_ = slice(dk * h, dk * (h + 1))
        bcum = bc_scr[b, :, ks_]
        b_last = bcum[c - 1:c, :]
        k_tail = _bf(k_ref[b, :, ks_] * jnp.exp(b_last - bcum))
        i = b * GLA_HEADS + h
        s_scr[i] = s_scr[i] * jnp.exp(b_last) + _dot_tn(_bf(v_ref[b, :, dv * h:dv * (h + 1)]), k_tail)
    for b, h in chains:
        vs_ = slice(dv * h, dv * (h + 1))
        oh = o[b, h]
        nrm = oh * lax.rsqrt(jnp.mean(oh * oh, axis=-1, keepdims=True) + RMS_EPS)
        o_ref[b, :, vs_] = _bf(nrm * nw * _silu(z_ref[b, :, vs_]))


def _gla_core_call(u, u_gate, w_up, b_alpha, norm_w, batch, seq, layer, w_ff1, w_ff2):
    c = CHUNK
    u3 = u.reshape(batch, seq, u.shape[1])
    g3 = u_gate.reshape(batch, seq, u_gate.shape[1])
    const2 = lambda t: (0, 0)
    job = _CastJob(layer, w_ff1, w_ff2, seq // c, lambda t: t)
    out, w1b, w2b = pl.pallas_call(
        _gla_kernel,
        grid=(seq // c,),
        in_specs=[
            pl.BlockSpec((batch, c, GLA_KEY_DIM), lambda t: (0, t, 0)),
            pl.BlockSpec((batch, c, GLA_KEY_DIM), lambda t: (0, t, 1)),
            pl.BlockSpec((batch, c, GLA_VALUE_DIM), lambda t: (0, t, 1)),
            pl.BlockSpec((batch, c, GLA_VALUE_DIM), lambda t: (0, t, 2)),
            pl.BlockSpec((batch, c, LANES), lambda t: (0, t, 0)),
            pl.BlockSpec((LANES, GLA_KEY_DIM), const2),
            pl.BlockSpec((1, GLA_KEY_DIM), const2),
            pl.BlockSpec((1, GLA_HEAD_V), const2),
        ] + job.in_specs,
        out_specs=[pl.BlockSpec((batch, c, GLA_VALUE_DIM), lambda t: (0, t, 0))] + job.out_specs,
        out_shape=[jax.ShapeDtypeStruct((batch, seq, GLA_VALUE_DIM), BF16)] + job.out_shape,
        scratch_shapes=[pltpu.VMEM((batch * GLA_HEADS, GLA_HEAD_V, GLA_HEAD_K), F32),
                        pltpu.VMEM((batch, c, GLA_KEY_DIM), F32)],
        compiler_params=_params("arbitrary"),
        name="gla_core",
    )(u3, u3, u3, u3, g3, w_up, b_alpha, norm_w, *job.args)
    return out.reshape(batch * seq, GLA_VALUE_DIM), w1b, w2b


def _pad_cols(w, total):
    return jnp.pad(w, ((0, 0), (0, total - w.shape[1])))


def _rglru_gate_weights(w_rgate, b_rgate, w_igate, b_igate):
    per = RNN_GROUP // RNN_BLOCK_DIM
    eye = jnp.eye(per, dtype=F32)

    def dense(w):
        w = w.reshape(RNN_GROUPS, per, RNN_BLOCK_DIM, RNN_BLOCK_DIM)
        return jnp.einsum("gjde,jk->gjdke", w, eye).reshape(RNN_GROUPS, RNN_GROUP, RNN_GROUP)

    wg = jnp.concatenate([dense(w_rgate), dense(w_igate)], axis=-1).astype(BF16)
    bg = jnp.concatenate([b_rgate.reshape(RNN_GROUPS, 1, RNN_GROUP),
                          b_igate.reshape(RNN_GROUPS, 1, RNN_GROUP)], axis=-1)
    return wg, bg


def _gdn_gate_weights(w_in):
    main = GDN_CONV_DIM + GDN_VALUE_DIM
    b_w = _pad_cols(w_in[:, main:main + GDN_V_HEADS], LANES)
    a_w = _pad_cols(w_in[:, main + GDN_V_HEADS:main + 2 * GDN_V_HEADS], LANES)
    return jnp.concatenate([b_w, a_w], axis=1)


def _gla_gate_weights(w_in):
    main = 2 * GLA_KEY_DIM + 2 * GLA_VALUE_DIM
    return _pad_cols(w_in[:, main:main + GLA_GATE_RANK], LANES)


def kernel(x, c, ln_g, ln_b, w_mod, b_mod, w_ff1, w_ff2, rglru_w_in, rglru_conv_w, rglru_conv_b, rglru_w_rgate, rglru_b_rgate, rglru_w_igate, rglru_b_igate, rglru_lambda, rglru_w_out, gdn_w_in, gdn_conv_w, gdn_a_log, gdn_dt_bias, gdn_norm_w, gdn_w_out, gla_w_in, gla_w_alpha_up, gla_b_alpha, gla_norm_w, gla_w_out):
    batch, seq, d = x.shape
    depth = w_mod.shape[0]
    n = batch * seq
    xf = x.reshape(n, d)

    assert batch <= MOD_ROWS
    c_pad = jnp.pad(c, ((0, MOD_ROWS - batch), (0, 0)))
    modv = _mod_call(c_pad, w_mod, b_mod).reshape(depth * MOD_ROWS * N_MOD, 1, d)
    lng = ln_g.reshape(depth * 2, 1, d)
    lnb = ln_b.reshape(depth * 2, 1, d)
    rglru_w_in_b = rglru_w_in.astype(BF16)
    gdn_w_in_b = gdn_w_in.astype(BF16)
    gla_w_in_b = gla_w_in.astype(BF16)
    rglru_w_out_b = rglru_w_out.astype(BF16)
    gdn_w_out_b = gdn_w_out.astype(BF16)
    gla_w_out_b = gla_w_out.astype(BF16)
    gdn_main = GDN_CONV_DIM + GDN_VALUE_DIM
    gla_main = 2 * GLA_KEY_DIM + 2 * GLA_VALUE_DIM

    for i in range(depth):
        kind, slot = i % N_MIXERS, i // N_MIXERS
        if kind == 0:
            u, _ = _proj_call(xf, modv, i, rglru_w_in_b, slot, 2 * RNN_WIDTH, seq)
            wg, bg = _rglru_gate_weights(rglru_w_rgate[slot], rglru_b_rgate[slot],
                                         rglru_w_igate[slot], rglru_b_igate[slot])
            y, w1b, w2b = _rglru_core_call(u, rglru_conv_w[slot], rglru_conv_b[slot][None], wg, bg,
                                           rglru_lambda[slot][None], batch, seq, i, w_ff1, w_ff2)
            w_out = rglru_w_out_b
        elif kind == 1:
            u, u_gate = _proj_call(xf, modv, i, gdn_w_in_b, slot, gdn_main, seq,
                                   w_small=_gdn_gate_weights(gdn_w_in_b[slot]))
            y, w1b, w2b = _gdn_core_call(u, u_gate, gdn_conv_w[slot],
                                         _pad_cols(gdn_a_log[slot][None], LANES),
                                         _pad_cols(gdn_dt_bias[slot][None], LANES),
                                         gdn_norm_w[slot][None], batch, seq, i, w_ff1, w_ff2)
            w_out = gdn_w_out_b
        else:
            u, u_gate = _proj_call(xf, modv, i, gla_w_in_b, slot, gla_main, seq,
                                   w_small=_gla_gate_weights(gla_w_in_b[slot]))
            w_up = jnp.pad(gla_w_alpha_up[slot], ((0, LANES - GLA_GATE_RANK), (0, 0))).astype(BF16)
            y, w1b, w2b = _gla_core_call(u, u_gate, w_up, gla_b_alpha[slot][None], gla_norm_w[slot][None],
                                         batch, seq, i, w_ff1, w_ff2)
            w_out = gla_w_out_b
        xf = _outln_call(y, w_out, slot, xf, modv, lng, lnb, i, seq)
        xf = _mlp_call(xf, modv, w1b, w2b, lng, lnb, i, seq)
    return xf.reshape(batch, seq, d)
```

```python
import functools

import jax
import jax.numpy as jnp
from jax import lax
from jax.experimental import pallas as pl
from jax.experimental.pallas import tpu as pltpu

F32 = jnp.float32
BF16 = jnp.bfloat16

D_MODEL = 2048
DEPTH = 4
N_MIXERS = 3
DEEPNORM_ALPHA = float((2 * DEPTH) ** 0.25)
LN_EPS = 1e-5
RMS_EPS = 1e-6
D_FF = 4 * D_MODEL
N_MOD = 6
MOD_ROWS = 8
CONV_WIDTH = 4

RNN_WIDTH = (5 * D_MODEL) // 4
RNN_BLOCKS = 16
RNN_BLOCK_DIM = RNN_WIDTH // RNN_BLOCKS
RNN_GROUP = 640
RNN_GROUPS = RNN_WIDTH // RNN_GROUP
LRU_C = 8.0

GDN_HEAD_DIM = 128
GDN_QK_HEADS = D_MODEL // GDN_HEAD_DIM
GDN_V_HEADS = 2 * GDN_QK_HEADS
GDN_KEY_DIM = GDN_QK_HEADS * GDN_HEAD_DIM
GDN_VALUE_DIM = GDN_V_HEADS * GDN_HEAD_DIM
GDN_CONV_DIM = 2 * GDN_KEY_DIM + GDN_VALUE_DIM
GDN_PAIRS = GDN_QK_HEADS
GDN_GROUP = 16

GLA_HEADS = 4
GLA_KEY_DIM = D_MODEL // 2
GLA_VALUE_DIM = D_MODEL
GLA_HEAD_K = GLA_KEY_DIM // GLA_HEADS
GLA_HEAD_V = GLA_VALUE_DIM // GLA_HEADS
GLA_GATE_RANK = 16
GLA_GATE_NORMALIZER = 16.0

CHUNK = 64
LANES = 128
SUBLANES = 8
OUTLN_SUBTILES = 4
VMEM_LIMIT = 56 * 1024 * 1024


def _dot(a, b):
    return jnp.dot(a, b, preferred_element_type=F32)


def _dot_nt(a, b):
    return lax.dot_general(a, b, (((1,), (1,)), ((), ())), preferred_element_type=F32)


def _dot_tn(a, b):
    return lax.dot_general(a, b, (((0,), (0,)), ((), ())), preferred_element_type=F32)


def _bf(x):
    return x.astype(BF16)


def _split3(x):
    hi = _bf(x)
    r1 = x - hi.astype(F32)
    mid = _bf(r1)
    lo = _bf(r1 - mid.astype(F32))
    return hi, mid, lo


def _dot_mask_lhs(m, x):
    hi, mid, lo = _split3(x)
    return _dot(m, hi) + _dot(m, mid) + _dot(m, lo)


def _dot_mask_rhs(x, m):
    hi, mid, lo = _split3(x)
    return _dot(hi, m) + _dot(mid, m) + _dot(lo, m)


def _softplus(x):
    return jnp.maximum(x, 0.0) + jnp.log1p(jnp.exp(-jnp.abs(x)))


def _sigmoid(x):
    return 0.5 * jnp.tanh(0.5 * x) + 0.5


def _silu(x):
    return x * _sigmoid(x)


def _params(*sem):
    return pltpu.CompilerParams(dimension_semantics=sem, vmem_limit_bytes=VMEM_LIMIT)


def _causal_conv(prev, x, cw_ref, cs):
    x_ext = jnp.concatenate([prev, x], axis=0)
    acc = cw_ref[CONV_WIDTH - 1:CONV_WIDTH, cs] * x
    for s in range(1, CONV_WIDTH):
        shifted = pltpu.roll(x_ext, s, 0)[SUBLANES:]
        acc = acc + cw_ref[CONV_WIDTH - 1 - s:CONV_WIDTH - s, cs] * shifted
    return acc


def _layer_norm_rows(r, g, b):
    mu = jnp.mean(r, axis=-1, keepdims=True)
    d = r - mu
    var = jnp.mean(d * d, axis=-1, keepdims=True)
    return d * lax.rsqrt(var + LN_EPS) * g + b


def _mod_kernel(c_ref, w_ref, b_ref, o_ref):
    c = c_ref[...]
    o_ref[0] = _dot(_bf(_silu(c)), _bf(w_ref[0])) + b_ref[0]


def _mod_call(c_pad, w_mod, b_mod, depth):
    _, d, n = w_mod.shape
    rows = c_pad.shape[0]
    tn = 1024
    return pl.pallas_call(
        _mod_kernel,
        grid=(depth, n // tn),
        in_specs=[
            pl.BlockSpec((rows, d), lambda l, j: (0, 0)),
            pl.BlockSpec((1, d, tn), lambda l, j: (l, 0, j)),
            pl.BlockSpec((1, 1, tn), lambda l, j: (l, 0, j)),
        ],
        out_specs=pl.BlockSpec((1, rows, tn), lambda l, j: (l, 0, j)),
        out_shape=jax.ShapeDtypeStruct((depth, rows, n), F32),
        compiler_params=_params("arbitrary", "arbitrary"),
        name="mod",
    )(c_pad, w_mod, b_mod.reshape(-1, 1, n))


def _mod_spec(layer, which, tpb, d):
    return pl.BlockSpec((None, 1, d),
                        lambda i, *_: ((layer * MOD_ROWS + i // tpb) * N_MOD + which, 0, 0))


def _ln_spec(layer, which, d):
    return pl.BlockSpec((None, 1, d), lambda i, *_: (2 * layer + which, 0, 0))


def _proj_kernel(has_small, x_ref, sh_ref, sc_ref, w_ref, *rest):
    if has_small:
        ws_ref, o_ref, os_ref, h_scr = rest
    else:
        o_ref, h_scr = rest

    @pl.when(pl.program_id(1) == 0)
    def _():
        h = _bf(x_ref[...] * (1.0 + sc_ref[...]) + sh_ref[...])
        h_scr[...] = h
        if has_small:
            os_ref[...] = _dot(h, ws_ref[...])

    o_ref[...] = _dot(h_scr[...], w_ref[...])


def _proj_tn(nout):
    return next(t for t in (1280, 1024) if nout % t == 0)


def _proj_call(x, modv, layer, w, slot, nout, seq, w_small=None):
    n, d = x.shape
    tm = min(1024, seq)
    tn = _proj_tn(nout)
    tpb = seq // tm
    has_small = w_small is not None
    in_specs = [
        pl.BlockSpec((tm, d), lambda i, j: (i, 0)),
        _mod_spec(layer, 0, tpb, d),
        _mod_spec(layer, 1, tpb, d),
        pl.BlockSpec((None, d, tn), lambda i, j: (slot, 0, j)),
    ]
    out_specs = [pl.BlockSpec((tm, tn), lambda i, j: (i, j))]
    out_shape = [jax.ShapeDtypeStruct((n, nout), F32)]
    args = [x, modv, modv, w]
    if has_small:
        ns = w_small.shape[1]
        in_specs.append(pl.BlockSpec((d, ns), lambda i, j: (0, 0)))
        out_specs.append(pl.BlockSpec((tm, ns), lambda i, j: (i, 0)))
        out_shape.append(jax.ShapeDtypeStruct((n, ns), F32))
        args.append(w_small)
    outs = pl.pallas_call(
        functools.partial(_proj_kernel, has_small),
        grid=(n // tm, nout // tn),
        in_specs=in_specs,
        out_specs=out_specs,
        out_shape=out_shape,
        scratch_shapes=[pltpu.VMEM((tm, d), BF16)],
        compiler_params=_params("arbitrary", "arbitrary"),
        name="proj",
    )(*args)
    return tuple(outs) if has_small else (outs[0], None)


class _CastJob:
    def __init__(self, layer, w_ff1, w_ff2, steps, step_of):
        _, d, dff = w_ff1.shape
        nch = next(c for c in (128, 64, 32, 16, 8, 4, 2, 1) if c <= steps)
        r1, r2 = d // nch, dff // nch
        chunk = lambda *idx: jnp.minimum(step_of(*idx), nch - 1)
        self.args = [w_ff1, w_ff2]
        self.in_specs = [pl.BlockSpec((None, r1, dff), lambda *idx: (layer, chunk(*idx), 0)),
                         pl.BlockSpec((None, r2, d), lambda *idx: (layer, chunk(*idx), 0))]
        self.out_specs = [pl.BlockSpec((r1, dff), lambda *idx: (chunk(*idx), 0)),
                          pl.BlockSpec((r2, d), lambda *idx: (chunk(*idx), 0))]
        self.out_shape = [jax.ShapeDtypeStruct((d, dff), BF16), jax.ShapeDtypeStruct((dff, d), BF16)]

    @staticmethod
    def run(f1_ref, f2_ref, f1b_ref, f2b_ref):
        f1b_ref[...] = _bf(f1_ref[...])
        f2b_ref[...] = _bf(f2_ref[...])


def _outln_kernel(y_ref, w_ref, x_ref, gt_ref, g_ref, b_ref, o_ref):
    sub = y_ref.shape[0] // OUTLN_SUBTILES
    for s in range(OUTLN_SUBTILES):
        rs = slice(s * sub, (s + 1) * sub)
        r = DEEPNORM_ALPHA * x_ref[rs, :] + (1.0 + gt_ref[...]) * _dot(y_ref[rs, :], w_ref[...])
        o_ref[rs, :] = _layer_norm_rows(r, g_ref[...], b_ref[...])


def _outln_call(y, w, slot, x, modv, mlayer, lng, lnb, layer, seq):
    n, d = x.shape
    kdim = y.shape[1]
    tm = min(512, seq)
    tpb = seq // tm
    return pl.pallas_call(
        _outln_kernel,
        grid=(n // tm,),
        in_specs=[
            pl.BlockSpec((tm, kdim), lambda i: (i, 0)),
            pl.BlockSpec((None, kdim, d), lambda i: (slot, 0, 0), pipeline_mode=pl.Buffered(1)),
            pl.BlockSpec((tm, d), lambda i: (i, 0)),
            _mod_spec(mlayer, 2, tpb, d),
            _ln_spec(layer, 0, d),
            _ln_spec(layer, 0, d),
        ],
        out_specs=pl.BlockSpec((tm, d), lambda i: (i, 0)),
        out_shape=jax.ShapeDtypeStruct((n, d), F32),
        compiler_params=_params("arbitrary"),
        name="outln",
    )(y, w, x, modv, lng, lnb)


def _mlp_kernel(has_mod_job, x_ref, sh_ref, sc_ref, gt_ref, w1_ref, w2_ref, g_ref, b_ref, *rest):
    if has_mod_job:
        c_ref, wm_ref, bm_ref, o_ref, mo_ref, h_scr = rest
    else:
        o_ref, h_scr = rest
    k = pl.program_id(1)

    @pl.when(k == 0)
    def _():
        h_scr[...] = _bf(x_ref[...] * (1.0 + sc_ref[...]) + sh_ref[...])
        o_ref[...] = jnp.zeros_like(o_ref)

    hid = jnp.maximum(_dot(h_scr[...], w1_ref[...]), 0.0)
    o_ref[...] += _dot(_bf(hid * hid), w2_ref[...])
    if has_mod_job:
        mo_ref[...] = _dot(_bf(_silu(c_ref[...])), _bf(wm_ref[...])) + bm_ref[...]

    @pl.when(k == pl.num_programs(1) - 1)
    def _():
        r = DEEPNORM_ALPHA * x_ref[...] + (1.0 + gt_ref[...]) * o_ref[...]
        o_ref[...] = _layer_norm_rows(r, g_ref[...], b_ref[...])


def _mlp_call(x, modv, mlayer, w1, w2, lng, lnb, layer, seq, mod_job=None):
    n, d = x.shape
    dff = w1.shape[1]
    tm = min(512, seq)
    tf = 1024
    tpb = seq // tm
    ksteps = dff // tf
    in_specs = [
        pl.BlockSpec((tm, d), lambda i, k: (i, 0)),
        _mod_spec(mlayer, 3, tpb, d),
        _mod_spec(mlayer, 4, tpb, d),
        _mod_spec(mlayer, 5, tpb, d),
        pl.BlockSpec((d, tf), lambda i, k: (0, k)),
        pl.BlockSpec((tf, d), lambda i, k: (k, 0)),
        _ln_spec(layer, 1, d),
        _ln_spec(layer, 1, d),
    ]
    out_specs = [pl.BlockSpec((tm, d), lambda i, k: (i, 0))]
    out_shape = [jax.ShapeDtypeStruct((n, d), F32)]
    args = [x, modv, modv, modv, w1, w2, lng, lnb]
    if mod_job is not None:
        c_pad, w_mod, b_mod, first = mod_job
        nl = w_mod.shape[0] - first
        nmod = w_mod.shape[2]
        steps = (n // tm) * ksteps
        wcol = next(w for w in (384, 768, 1536, 3072, 6144, 12288) if nl * (nmod // w) <= steps)
        per = nmod // wcol
        q = lambda i, k: jnp.minimum(i * ksteps + k, nl * per - 1)
        rows = c_pad.shape[0]
        in_specs += [
            pl.BlockSpec((rows, d), lambda i, k: (0, 0)),
            pl.BlockSpec((None, d, wcol), lambda i, k: (first + q(i, k) // per, 0, q(i, k) % per)),
            pl.BlockSpec((None, 1, wcol), lambda i, k: (first + q(i, k) // per, 0, q(i, k) % per)),
        ]
        out_specs.append(pl.BlockSpec((None, rows, wcol), lambda i, k: (q(i, k) // per, 0, q(i, k) % per)))
        out_shape.append(jax.ShapeDtypeStruct((nl, rows, nmod), F32))
        args += [c_pad, w_mod, b_mod.reshape(-1, 1, nmod)]
    outs = pl.pallas_call(
        functools.partial(_mlp_kernel, mod_job is not None),
        grid=(n // tm, ksteps),
        in_specs=in_specs,
        out_specs=out_specs,
        out_shape=out_shape,
        scratch_shapes=[pltpu.VMEM((tm, d), BF16)],
        compiler_params=_params("arbitrary", "arbitrary"),
        name="mlp",
    )(*args)
    return (outs[0], outs[1]) if mod_job is not None else (outs[0], None)


def _rglru_kernel(gate_ref, rec_ref, cw_ref, cb_ref, wg_ref, bg_ref, lam_ref, f1_ref, f2_ref,
                  o_ref, f1b_ref, f2b_ref, halo, abuf, bbuf, hcar):
    tm = rec_ref.shape[0]
    _CastJob.run(f1_ref, f2_ref, f1b_ref, f2b_ref)

    @pl.when(pl.program_id(1) == 0)
    def _():
        halo[...] = jnp.zeros_like(halo)
        hcar[...] = jnp.zeros_like(hcar)

    sp = _softplus(-lam_ref[...])
    for g in range(RNN_GROUPS):
        cs = slice(RNN_GROUP * g, RNN_GROUP * (g + 1))
        xr = _causal_conv(halo[:, cs], rec_ref[:, cs], cw_ref, cs) + cb_ref[:, cs]
        pre = _dot(_bf(xr), wg_ref[g]) + bg_ref[g]
        r_gate = _sigmoid(pre[:, :RNN_GROUP])
        i_gate = _sigmoid(pre[:, RNN_GROUP:])
        neg_log_a = LRU_C * r_gate * sp[:, cs]
        a = jnp.exp(-neg_log_a)
        abuf[:, cs] = a
        bbuf[:, cs] = jnp.sqrt(jnp.tanh(neg_log_a) * (1.0 + a * a)) * (i_gate * xr)
    halo[...] = rec_ref[tm - SUBLANES:tm, :]

    row = lax.broadcasted_iota(jnp.int32, (SUBLANES, RNN_WIDTH), 0)

    def body(j, carry):
        r0 = pl.multiple_of(j * SUBLANES, SUBLANES)
        a = abuf[pl.ds(r0, SUBLANES), :]
        b = bbuf[pl.ds(r0, SUBLANES), :]
        for k in (1, 2, 4):
            a_s = jnp.where(row >= k, pltpu.roll(a, k, 0), 1.0)
            b_s = jnp.where(row >= k, pltpu.roll(b, k, 0), 0.0)
            b = a * b_s + b
            a = a * a_s
        h = a * carry + b
        bbuf[pl.ds(r0, SUBLANES), :] = h
        return h[SUBLANES - 1:SUBLANES, :]

    hcar[...] = lax.fori_loop(0, tm // SUBLANES, body, hcar[...])
    o_ref[...] = _bf(jax.nn.gelu(gate_ref[...], approximate=True) * bbuf[...])


def _rglru_core_call(u, conv_w, conv_b, wg, bg, lam, batch, seq, layer, w_ff1, w_ff2):
    n = u.shape[0]
    tm = min(256, seq)
    tpb = seq // tm
    r = RNN_WIDTH
    const2 = lambda b, t: (0, 0)
    job = _CastJob(layer, w_ff1, w_ff2, batch * tpb, lambda b, t: b * tpb + t)
    return pl.pallas_call(
        _rglru_kernel,
        grid=(batch, tpb),
        in_specs=[
            pl.BlockSpec((tm, r), lambda b, t: (b * tpb + t, 0)),
            pl.BlockSpec((tm, r), lambda b, t: (b * tpb + t, 1)),
            pl.BlockSpec((CONV_WIDTH, r), const2),
            pl.BlockSpec((1, r), const2),
            pl.BlockSpec((RNN_GROUPS, RNN_GROUP, 2 * RNN_GROUP), lambda b, t: (0, 0, 0)),
            pl.BlockSpec((RNN_GROUPS, 1, 2 * RNN_GROUP), lambda b, t: (0, 0, 0)),
            pl.BlockSpec((1, r), const2),
        ] + job.in_specs,
        out_specs=[pl.BlockSpec((tm, r), lambda b, t: (b * tpb + t, 0))] + job.out_specs,
        out_shape=[jax.ShapeDtypeStruct((n, r), BF16)] + job.out_shape,
        scratch_shapes=[
            pltpu.VMEM((SUBLANES, r), F32),
            pltpu.VMEM((tm, r), F32),
            pltpu.VMEM((tm, r), F32),
            pltpu.VMEM((1, r), F32),
        ],
        compiler_params=_params("arbitrary", "arbitrary"),
        name="rglru_core",
    )(u, u, conv_w, conv_b, wg, bg, lam, *job.args)


def _gdn_kernel(qkv_ref, z_ref, br_ref, ar_ref, cw_ref, alog_ref, dtb_ref, nw_ref, f1_ref, f2_ref,
                o_ref, f1b_ref, f2b_ref, halo, qs, ks, vs, gcb, btb, gct, s_scr, o_scr):
    c = CHUNK
    hd = GDN_HEAD_DIM
    _CastJob.run(f1_ref, f2_ref, f1b_ref, f2b_ref)

    @pl.when(pl.program_id(1) == 0)
    def _():
        halo[...] = jnp.zeros_like(halo)
        s_scr[...] = jnp.zeros_like(s_scr)

    beta = _sigmoid(br_ref[...])
    g = -jnp.exp(alog_ref[...]) * _softplus(ar_ref[...] + dtb_ref[...])
    ti64 = lax.broadcasted_iota(jnp.int32, (c, c), 0)
    tj64 = lax.broadcasted_iota(jnp.int32, (c, c), 1)
    tri = jnp.where(ti64 >= tj64, 1.0, 0.0).astype(BF16)
    gc = _dot_mask_lhs(tri, g)
    g_t = jnp.concatenate([g, jnp.zeros_like(g)], axis=0).T
    ut = lax.broadcasted_iota(jnp.int32, (2 * c, 2 * c), 0)
    uj = lax.broadcasted_iota(jnp.int32, (2 * c, 2 * c), 1) & (c - 1)
    upper2 = jnp.where((ut <= uj) & (ut < c), 1.0, 0.0).astype(BF16)
    gct[...] = _dot_mask_rhs(g_t, upper2)
    for h in range(GDN_V_HEADS):
        gcb[h] = jnp.broadcast_to(gc[:, h:h + 1], (c, LANES))
        btb[h] = jnp.broadcast_to(beta[:, h:h + 1], (c, LANES))

    def conv_silu(col0, width):
        cs = slice(col0, col0 + width)
        return _silu(_causal_conv(halo[:, cs], qkv_ref[:, cs], cw_ref, cs))

    for p in range(GDN_PAIRS):
        q = conv_silu(hd * p, hd)
        q = q * lax.rsqrt(jnp.sum(q * q, axis=-1, keepdims=True) + RMS_EPS) * (hd ** -0.5)
        k = conv_silu(GDN_KEY_DIM + hd * p, hd)
        k = k * lax.rsqrt(jnp.sum(k * k, axis=-1, keepdims=True) + RMS_EPS)
        qs[p] = q
        ks[p] = k
        vs[p] = conv_silu(2 * GDN_KEY_DIM + 2 * hd * p, 2 * hd)
    halo[...] = qkv_ref[c - SUBLANES:c, :]

    ti = lax.broadcasted_iota(jnp.int32, (c, LANES), 0)
    lane = lax.broadcasted_iota(jnp.int32, (c, LANES), 1)
    left = lane < c
    tj = lane & (c - 1)
    left_row = lax.broadcasted_iota(jnp.int32, (1, LANES), 1) < c
    eye = jnp.where(ti == tj, 1.0, 0.0)
    left2 = lax.broadcasted_iota(jnp.int32, (c, 2 * hd), 1) < hd
    left2s = lax.broadcasted_iota(jnp.int32, (hd, 2 * hd), 1) < hd
    zero_tile = jnp.zeros((c, hd), F32)

    def block_diag(x, msk):
        return jnp.concatenate([jnp.where(msk, x, 0.0), jnp.where(msk, 0.0, x)], axis=0)

    def gc_row(p):
        return jnp.where(left_row, gct[2 * p:2 * p + 1, :], gct[2 * p + 1:2 * p + 2, :])

    for g0 in range(0, GDN_PAIRS, GDN_GROUP):
        grp = range(g0, g0 + GDN_GROUP)
        kq, a_qk, w_pow, t_inv, wu, r1, r2, v_new = {}, {}, {}, {}, {}, {}, {}, {}
        for p in grp:
            k = ks[p]
            kq[p] = _dot_nt(_bf(jnp.concatenate([k, qs[p]], axis=0)),
                            _bf(jnp.concatenate([k, k], axis=0)))
        for p in grp:
            col = jnp.where(left, gcb[2 * p], gcb[2 * p + 1])
            decay = jnp.where(ti >= tj, jnp.exp(jnp.minimum(col - gc_row(p), 0.0)), 0.0)
            a_kk = (jnp.where(ti > tj, kq[p][:c] * decay, 0.0)
                    * jnp.where(left, btb[2 * p], btb[2 * p + 1]))
            a_qk[p] = kq[p][c:] * decay
            w_pow[p] = _dot(_bf(a_kk), _bf(block_diag(a_kk, left)))
            t_inv[p] = eye - a_kk
        for _ in range(4):
            for p in grp:
                r = _dot(_bf(jnp.concatenate([w_pow[p], t_inv[p]], axis=0)),
                         _bf(block_diag(w_pow[p], left)))
                w_pow[p] = r[:c]
                t_inv[p] = t_inv[p] + r[c:]
        for p in grp:
            t_inv[p] = t_inv[p] + _dot(_bf(t_inv[p]), _bf(block_diag(w_pow[p], left)))
        for p in grp:
            k = ks[p]
            bt0 = btb[2 * p]
            bt1 = btb[2 * p + 1]
            v_ss = vs[p]
            rhs = jnp.concatenate([
                jnp.concatenate([k * (bt0 * jnp.exp(gcb[2 * p])), zero_tile,
                                 v_ss[:, :hd] * bt0, zero_tile], axis=1),
                jnp.concatenate([zero_tile, k * (bt1 * jnp.exp(gcb[2 * p + 1])),
                                 zero_tile, v_ss[:, hd:] * bt1], axis=1)], axis=0)
            wu[p] = _dot(_bf(t_inv[p]), _bf(rhs))
        for p in grp:
            q = qs[p]
            q_dec = jnp.concatenate([q * jnp.exp(gcb[2 * p]), q * jnp.exp(gcb[2 * p + 1])], axis=1)
            r1[p] = _dot(_bf(jnp.concatenate([wu[p][:, :2 * hd], q_dec], axis=0)),
                         _bf(block_diag(s_scr[p], left2s)))
        for p in grp:
            v_new[p] = wu[p][:, 2 * hd:] - r1[p][:c]
            k = ks[p]
            gl = jnp.where(left_row, gcb[2 * p][c - 1:c, :], gcb[2 * p + 1][c - 1:c, :])
            k_tail = jnp.concatenate([k, k], axis=0).T * jnp.exp(gl - gc_row(p))
            r2[p] = _dot(_bf(jnp.concatenate([a_qk[p], k_tail], axis=0)),
                         _bf(block_diag(v_new[p], left2)))
        for p in grp:
            o_scr[p] = r1[p][c:] + r2[p][:c]
            cd = jnp.concatenate([jnp.exp(gcb[2 * p][c - 1:c, :]),
                                  jnp.exp(gcb[2 * p + 1][c - 1:c, :])], axis=1)
            s_scr[p] = s_scr[p] * cd + r2[p][c:]

    nw = nw_ref[...]
    for p in range(GDN_PAIRS):
        o_ss = o_scr[p]
        for s in range(2):
            cs = slice(2 * hd * p + hd * s, 2 * hd * p + hd * (s + 1))
            o = o_ss[:, hd * s:hd * (s + 1)]
            nrm = o * lax.rsqrt(jnp.mean(o * o, axis=-1, keepdims=True) + RMS_EPS)
            o_ref[:, cs] = _bf(nrm * nw * _silu(z_ref[:, cs]))


def _gdn_core_call(u, u_gate, conv_w, a_log, dt_bias, norm_w, batch, seq, layer, w_ff1, w_ff2):
    n = u.shape[0]
    c = CHUNK
    cpb = seq // c
    hd = GDN_HEAD_DIM
    row = lambda b, t: b * cpb + t
    const2 = lambda b, t: (0, 0)
    job = _CastJob(layer, w_ff1, w_ff2, batch * cpb, row)
    return pl.pallas_call(
        _gdn_kernel,
        grid=(batch, cpb),
        in_specs=[
            pl.BlockSpec((c, GDN_CONV_DIM), lambda b, t: (row(b, t), 0)),
            pl.BlockSpec((c, GDN_VALUE_DIM), lambda b, t: (row(b, t), GDN_CONV_DIM // GDN_VALUE_DIM)),
            pl.BlockSpec((c, LANES), lambda b, t: (row(b, t), 0)),
            pl.BlockSpec((c, LANES), lambda b, t: (row(b, t), 1)),
            pl.BlockSpec((CONV_WIDTH, GDN_CONV_DIM), const2),
            pl.BlockSpec((1, LANES), const2),
            pl.BlockSpec((1, LANES), const2),
            pl.BlockSpec((1, hd), const2),
        ] + job.in_specs,
        out_specs=[pl.BlockSpec((c, GDN_VALUE_DIM), lambda b, t: (row(b, t), 0))] + job.out_specs,
        out_shape=[jax.ShapeDtypeStruct((n, GDN_VALUE_DIM), BF16)] + job.out_shape,
        scratch_shapes=[
            pltpu.VMEM((SUBLANES, GDN_CONV_DIM), F32),
            pltpu.VMEM((GDN_PAIRS, c, hd), F32),
            pltpu.VMEM((GDN_PAIRS, c, hd), F32),
            pltpu.VMEM((GDN_PAIRS, c, 2 * hd), F32),
            pltpu.VMEM((GDN_V_HEADS, c, LANES), F32),
            pltpu.VMEM((GDN_V_HEADS, c, LANES), F32),
            pltpu.VMEM((2 * c, 2 * c), F32),
            pltpu.VMEM((GDN_PAIRS, hd, 2 * hd), F32),
            pltpu.VMEM((GDN_PAIRS, c, 2 * hd), F32),
        ],
        compiler_params=_params("arbitrary", "arbitrary"),
        name="gdn_core",
    )(u, u, u_gate, u_gate, conv_w, a_log, dt_bias, norm_w, *job.args)


def _gla_kernel(q_ref, k_ref, v_ref, z_ref, al_ref, wup_ref, bal_ref, nw_ref, f1_ref, f2_ref,
                o_ref, f1b_ref, f2b_ref, s_scr, bc_scr):
    c = CHUNK
    dk = GLA_HEAD_K
    dv = GLA_HEAD_V
    _CastJob.run(f1_ref, f2_ref, f1b_ref, f2b_ref)

    nb = q_ref.shape[0]

    @pl.when(pl.program_id(0) == 0)
    def _():
        s_scr[...] = jnp.zeros_like(s_scr)

    ti = lax.broadcasted_iota(jnp.int32, (c, c), 0)
    tj = lax.broadcasted_iota(jnp.int32, (c, c), 1)
    causal = ti >= tj
    tri = jnp.where(causal, 1.0, 0.0).astype(BF16)
    nw = nw_ref[...]
    chains = [(b, h) for b in range(nb) for h in range(GLA_HEADS)]
    pre = {}
    for b in range(nb):
        pre[b] = _dot(_bf(al_ref[b]), wup_ref[...]) + bal_ref[...]
    for b in range(nb):
        log_alpha = (-1.0 / GLA_GATE_NORMALIZER) * _softplus(-pre[b])
        bc_scr[b] = _dot_mask_lhs(tri, log_alpha)
    q_dec, a_qk, o = {}, {}, {}
    for b, h in chains:
        ks_ = slice(dk * h, dk * (h + 1))
        bcum = bc_scr[b, :, ks_]
        q_dec[b, h] = _bf(q_ref[b, :, ks_] * (dk ** -0.5) * jnp.exp(bcum))
        a_qk[b, h] = _dot_nt(q_dec[b, h], _bf(k_ref[b, :, ks_] * jnp.exp(-bcum)))
    for b, h in chains:
        v = _bf(v_ref[b, :, dv * h:dv * (h + 1)])
        s_t = s_scr[b * GLA_HEADS + h]
        o[b, h] = (_dot_nt(q_dec[b, h], _bf(s_t))
                   + _dot(_bf(jnp.where(causal, a_qk[b, h], 0.0)), v))
    for b, h in chains:
        ks_ = slice(dk * h, dk * (h + 1))
        bcum = bc_scr[b, :, ks_]
        b_last = bcum[c - 1:c, :]
        k_tail = _bf(k_ref[b, :, ks_] * jnp.exp(b_last - bcum))
        i = b * GLA_HEADS + h
        s_scr[i] = s_scr[i] * jnp.exp(b_last) + _dot_tn(_bf(v_ref[b, :, dv * h:dv * (h + 1)]), k_tail)
    for b, h in chains:
        vs_ = slice(dv * h, dv * (h + 1))
        oh = o[b, h]
        nrm = oh * lax.rsqrt(jnp.mean(oh * oh, axis=-1, keepdims=True) + RMS_EPS)
        o_ref[b, :, vs_] = _bf(nrm * nw * _silu(z_ref[b, :, vs_]))


def _gla_core_call(u, u_gate, w_up, b_alpha, norm_w, batch, seq, layer, w_ff1, w_ff2):
    c = CHUNK
    u3 = u.reshape(batch, seq, u.shape[1])
    g3 = u_gate.reshape(batch, seq, u_gate.shape[1])
    const2 = lambda t: (0, 0)
    job = _CastJob(layer, w_ff1, w_ff2, seq // c, lambda t: t)
    out, w1b, w2b = pl.pallas_call(
        _gla_kernel,
        grid=(seq // c,),
        in_specs=[
            pl.BlockSpec((batch, c, GLA_KEY_DIM), lambda t: (0, t, 0)),
            pl.BlockSpec((batch, c, GLA_KEY_DIM), lambda t: (0, t, 1)),
            pl.BlockSpec((batch, c, GLA_VALUE_DIM), lambda t: (0, t, 1)),
            pl.BlockSpec((batch, c, GLA_VALUE_DIM), lambda t: (0, t, 2)),
            pl.BlockSpec((batch, c, LANES), lambda t: (0, t, 0)),
            pl.BlockSpec((LANES, GLA_KEY_DIM), const2),
            pl.BlockSpec((1, GLA_KEY_DIM), const2),
            pl.BlockSpec((1, GLA_HEAD_V), const2),
        ] + job.in_specs,
        out_specs=[pl.BlockSpec((batch, c, GLA_VALUE_DIM), lambda t: (0, t, 0))] + job.out_specs,
        out_shape=[jax.ShapeDtypeStruct((batch, seq, GLA_VALUE_DIM), BF16)] + job.out_shape,
        scratch_shapes=[pltpu.VMEM((batch * GLA_HEADS, GLA_HEAD_V, GLA_HEAD_K), F32),
                        pltpu.VMEM((batch, c, GLA_KEY_DIM), F32)],
        compiler_params=_params("arbitrary"),
        name="gla_core",
    )(u3, u3, u3, u3, g3, w_up, b_alpha, norm_w, *job.args)
    return out.reshape(batch * seq, GLA_VALUE_DIM), w1b, w2b


def _pad_cols(w, total):
    return jnp.pad(w, ((0, 0), (0, total - w.shape[1])))


def _rglru_gate_weights(w_rgate, b_rgate, w_igate, b_igate):
    per = RNN_GROUP // RNN_BLOCK_DIM
    eye = jnp.eye(per, dtype=F32)

    def dense(w):
        w = w.reshape(RNN_GROUPS, per, RNN_BLOCK_DIM, RNN_BLOCK_DIM)
        return jnp.einsum("gjde,jk->gjdke", w, eye).reshape(RNN_GROUPS, RNN_GROUP, RNN_GROUP)

    wg = jnp.concatenate([dense(w_rgate), dense(w_igate)], axis=-1).astype(BF16)
    bg = jnp.concatenate([b_rgate.reshape(RNN_GROUPS, 1, RNN_GROUP),
                          b_igate.reshape(RNN_GROUPS, 1, RNN_GROUP)], axis=-1)
    return wg, bg


def _gdn_gate_weights(w_in):
    main = GDN_CONV_DIM + GDN_VALUE_DIM
    b_w = _pad_cols(w_in[:, main:main + GDN_V_HEADS], LANES)
    a_w = _pad_cols(w_in[:, main + GDN_V_HEADS:main + 2 * GDN_V_HEADS], LANES)
    return jnp.concatenate([b_w, a_w], axis=1)


def _gla_gate_weights(w_in):
    main = 2 * GLA_KEY_DIM + 2 * GLA_VALUE_DIM
    return _pad_cols(w_in[:, main:main + GLA_GATE_RANK], LANES)


def kernel(x, c, ln_g, ln_b, w_mod, b_mod, w_ff1, w_ff2, rglru_w_in, rglru_conv_w, rglru_conv_b, rglru_w_rgate, rglru_b_rgate, rglru_w_igate, rglru_b_igate, rglru_lambda, rglru_w_out, gdn_w_in, gdn_conv_w, gdn_a_log, gdn_dt_bias, gdn_norm_w, gdn_w_out, gla_w_in, gla_w_alpha_up, gla_b_alpha, gla_norm_w, gla_w_out):
    batch, seq, d = x.shape
    depth = w_mod.shape[0]
    n = batch * seq
    xf = x.reshape(n, d)

    assert batch <= MOD_ROWS
    c_pad = jnp.pad(c, ((0, MOD_ROWS - batch), (0, 0)))
    modv = _mod_call(c_pad, w_mod, b_mod, 1).reshape(MOD_ROWS * N_MOD, 1, d)
    mlayer = 0
    lng = ln_g.reshape(depth * 2, 1, d)
    lnb = ln_b.reshape(depth * 2, 1, d)
    rglru_w_in_b = rglru_w_in.astype(BF16)
    gdn_w_in_b = gdn_w_in.astype(BF16)
    gla_w_in_b = gla_w_in.astype(BF16)
    rglru_w_out_b = rglru_w_out.astype(BF16)
    gdn_w_out_b = gdn_w_out.astype(BF16)
    gla_w_out_b = gla_w_out.astype(BF16)
    gdn_main = GDN_CONV_DIM + GDN_VALUE_DIM
    gla_main = 2 * GLA_KEY_DIM + 2 * GLA_VALUE_DIM

    for i in range(depth):
        kind, slot = i % N_MIXERS, i // N_MIXERS
        if kind == 0:
            u, _ = _proj_call(xf, modv, mlayer, rglru_w_in_b, slot, 2 * RNN_WIDTH, seq)
            wg, bg = _rglru_gate_weights(rglru_w_rgate[slot], rglru_b_rgate[slot],
                                         rglru_w_igate[slot], rglru_b_igate[slot])
            y, w1b, w2b = _rglru_core_call(u, rglru_conv_w[slot], rglru_conv_b[slot][None], wg, bg,
                                           rglru_lambda[slot][None], batch, seq, i, w_ff1, w_ff2)
            w_out = rglru_w_out_b
        elif kind == 1:
            u, u_gate = _proj_call(xf, modv, mlayer, gdn_w_in_b, slot, gdn_main, seq,
                                   w_small=_gdn_gate_weights(gdn_w_in_b[slot]))
            y, w1b, w2b = _gdn_core_call(u, u_gate, gdn_conv_w[slot],
                                         _pad_cols(gdn_a_log[slot][None], LANES),
                                         _pad_cols(gdn_dt_bias[slot][None], LANES),
                                         gdn_norm_w[slot][None], batch, seq, i, w_ff1, w_ff2)
            w_out = gdn_w_out_b
        else:
            u, u_gate = _proj_call(xf, modv, mlayer, gla_w_in_b, slot, gla_main, seq,
                                   w_small=_gla_gate_weights(gla_w_in_b[slot]))
            w_up = jnp.pad(gla_w_alpha_up[slot], ((0, LANES - GLA_GATE_RANK), (0, 0))).astype(BF16)
            y, w1b, w2b = _gla_core_call(u, u_gate, w_up, gla_b_alpha[slot][None], gla_norm_w[slot][None],
                                         batch, seq, i, w_ff1, w_ff2)
            w_out = gla_w_out_b
        xf = _outln_call(y, w_out, slot, xf, modv, mlayer, lng, lnb, i, seq)
        job = (c_pad, w_mod, b_mod, 1) if (i == 0 and depth > 1) else None
        xf, mod_rest = _mlp_call(xf, modv, mlayer, w1b, w2b, lng, lnb, i, seq, mod_job=job)
        if job is not None:
            modv = mod_rest.reshape((depth - 1) * MOD_ROWS * N_MOD, 1, d)
        mlayer = i
    return xf.reshape(batch, seq, d)
```

```python
import functools

import jax
import jax.numpy as jnp
from jax import lax
from jax.experimental import pallas as pl
from jax.experimental.pallas import tpu as pltpu

F32 = jnp.float32
BF16 = jnp.bfloat16

D_MODEL = 2048
DEPTH = 4
N_MIXERS = 3
DEEPNORM_ALPHA = float((2 * DEPTH) ** 0.25)
LN_EPS = 1e-5
RMS_EPS = 1e-6
D_FF = 4 * D_MODEL
N_MOD = 6
MOD_ROWS = 8
CONV_WIDTH = 4

RNN_WIDTH = (5 * D_MODEL) // 4
RNN_BLOCKS = 16
RNN_BLOCK_DIM = RNN_WIDTH // RNN_BLOCKS
RNN_GROUP = 640
RNN_GROUPS = RNN_WIDTH // RNN_GROUP
LRU_C = 8.0

GDN_HEAD_DIM = 128
GDN_QK_HEADS = D_MODEL // GDN_HEAD_DIM
GDN_V_HEADS = 2 * GDN_QK_HEADS
GDN_KEY_DIM = GDN_QK_HEADS * GDN_HEAD_DIM
GDN_VALUE_DIM = GDN_V_HEADS * GDN_HEAD_DIM
GDN_CONV_DIM = 2 * GDN_KEY_DIM + GDN_VALUE_DIM
GDN_PAIRS = GDN_QK_HEADS
GDN_GROUP = 16

GLA_HEADS = 4
GLA_KEY_DIM = D_MODEL // 2
GLA_VALUE_DIM = D_MODEL
GLA_HEAD_K = GLA_KEY_DIM // GLA_HEADS
GLA_HEAD_V = GLA_VALUE_DIM // GLA_HEADS
GLA_GATE_RANK = 16
GLA_GATE_NORMALIZER = 16.0

CHUNK = 64
LANES = 128
SUBLANES = 8
OUTLN_SUBTILES = 4
VMEM_LIMIT = 56 * 1024 * 1024


def _dot(a, b):
    return jnp.dot(a, b, preferred_element_type=F32)


def _dot_nt(a, b):
    return lax.dot_general(a, b, (((1,), (1,)), ((), ())), preferred_element_type=F32)


def _dot_tn(a, b):
    return lax.dot_general(a, b, (((0,), (0,)), ((), ())), preferred_element_type=F32)


def _bf(x):
    return x.astype(BF16)


def _split3(x):
    hi = _bf(x)
    r1 = x - hi.astype(F32)
    mid = _bf(r1)
    lo = _bf(r1 - mid.astype(F32))
    return hi, mid, lo


def _dot_mask_lhs(m, x):
    hi, mid, lo = _split3(x)
    return _dot(m, hi) + _dot(m, mid) + _dot(m, lo)


def _dot_mask_rhs(x, m):
    hi, mid, lo = _split3(x)
    return _dot(hi, m) + _dot(mid, m) + _dot(lo, m)


def _softplus(x):
    return jnp.maximum(x, 0.0) + jnp.log1p(jnp.exp(-jnp.abs(x)))


def _sigmoid(x):
    return 0.5 * jnp.tanh(0.5 * x) + 0.5


def _silu(x):
    return x * _sigmoid(x)


def _params(*sem):
    return pltpu.CompilerParams(dimension_semantics=sem, vmem_limit_bytes=VMEM_LIMIT)


def _causal_conv(prev, x, cw_ref, cs):
    x_ext = jnp.concatenate([prev, x], axis=0)
    acc = cw_ref[CONV_WIDTH - 1:CONV_WIDTH, cs] * x
    for s in range(1, CONV_WIDTH):
        shifted = pltpu.roll(x_ext, s, 0)[SUBLANES:]
        acc = acc + cw_ref[CONV_WIDTH - 1 - s:CONV_WIDTH - s, cs] * shifted
    return acc


def _layer_norm_rows(r, g, b):
    mu = jnp.mean(r, axis=-1, keepdims=True)
    d = r - mu
    var = jnp.mean(d * d, axis=-1, keepdims=True)
    return d * lax.rsqrt(var + LN_EPS) * g + b


def _mod_kernel(c_ref, w_ref, b_ref, o_ref):
    c = c_ref[...]
    o_ref[0] = _dot(_bf(_silu(c)), _bf(w_ref[0])) + b_ref[0]


def _mod_call(c_pad, w_mod, b_mod, depth):
    _, d, n = w_mod.shape
    rows = c_pad.shape[0]
    tn = 1024
    return pl.pallas_call(
        _mod_kernel,
        grid=(depth, n // tn),
        in_specs=[
            pl.BlockSpec((rows, d), lambda l, j: (0, 0)),
            pl.BlockSpec((1, d, tn), lambda l, j: (l, 0, j)),
            pl.BlockSpec((1, 1, tn), lambda l, j: (l, 0, j)),
        ],
        out_specs=pl.BlockSpec((1, rows, tn), lambda l, j: (l, 0, j)),
        out_shape=jax.ShapeDtypeStruct((depth, rows, n), F32),
        compiler_params=_params("arbitrary", "arbitrary"),
        name="mod",
    )(c_pad, w_mod, b_mod.reshape(-1, 1, n))


def _mod_spec(layer, which, tpb, d):
    return pl.BlockSpec((None, 1, d),
                        lambda i, *_: ((layer * MOD_ROWS + i // tpb) * N_MOD + which, 0, 0))


def _ln_spec(layer, which, d):
    return pl.BlockSpec((None, 1, d), lambda i, *_: (2 * layer + which, 0, 0))


def _proj_kernel(has_small, x_ref, sh_ref, sc_ref, w_ref, *rest):
    if has_small:
        ws_ref, o_ref, os_ref, h_scr = rest
    else:
        o_ref, h_scr = rest

    @pl.when(pl.program_id(1) == 0)
    def _():
        h = _bf(x_ref[...] * (1.0 + sc_ref[...]) + sh_ref[...])
        h_scr[...] = h
        if has_small:
            os_ref[...] = _dot(h, ws_ref[...])

    o_ref[...] = _dot(h_scr[...], w_ref[...])


def _proj_tn(nout):
    return next(t for t in (1536, 1280, 1024) if nout % t == 0)


def _proj_call(x, modv, layer, w, slot, nout, seq, w_small=None):
    n, d = x.shape
    tm = min(1024, seq)
    tn = _proj_tn(nout)
    tpb = seq // tm
    has_small = w_small is not None
    in_specs = [
        pl.BlockSpec((tm, d), lambda i, j: (i, 0)),
        _mod_spec(layer, 0, tpb, d),
        _mod_spec(layer, 1, tpb, d),
        pl.BlockSpec((None, d, tn), lambda i, j: (slot, 0, j)),
    ]
    out_specs = [pl.BlockSpec((tm, tn), lambda i, j: (i, j))]
    out_shape = [jax.ShapeDtypeStruct((n, nout), F32)]
    args = [x, modv, modv, w]
    if has_small:
        ns = w_small.shape[1]
        in_specs.append(pl.BlockSpec((d, ns), lambda i, j: (0, 0)))
        out_specs.append(pl.BlockSpec((tm, ns), lambda i, j: (i, 0)))
        out_shape.append(jax.ShapeDtypeStruct((n, ns), F32))
        args.append(w_small)
    outs = pl.pallas_call(
        functools.partial(_proj_kernel, has_small),
        grid=(n // tm, nout // tn),
        in_specs=in_specs,
        out_specs=out_specs,
        out_shape=out_shape,
        scratch_shapes=[pltpu.VMEM((tm, d), BF16)],
        compiler_params=_params("arbitrary", "arbitrary"),
        name="proj",
    )(*args)
    return tuple(outs) if has_small else (outs[0], None)


class _CastJob:
    def __init__(self, layer, w_ff1, w_ff2, steps, step_of):
        _, d, dff = w_ff1.shape
        nch = next(c for c in (128, 64, 32, 16, 8, 4, 2, 1) if c <= steps)
        r1, r2 = d // nch, dff // nch
        chunk = lambda *idx: jnp.minimum(step_of(*idx), nch - 1)
        self.args = [w_ff1, w_ff2]
        self.in_specs = [pl.BlockSpec((None, r1, dff), lambda *idx: (layer, chunk(*idx), 0)),
                         pl.BlockSpec((None, r2, d), lambda *idx: (layer, chunk(*idx), 0))]
        self.out_specs = [pl.BlockSpec((r1, dff), lambda *idx: (chunk(*idx), 0)),
                          pl.BlockSpec((r2, d), lambda *idx: (chunk(*idx), 0))]
        self.out_shape = [jax.ShapeDtypeStruct((d, dff), BF16), jax.ShapeDtypeStruct((dff, d), BF16)]

    @staticmethod
    def run(f1_ref, f2_ref, f1b_ref, f2b_ref):
        f1b_ref[...] = _bf(f1_ref[...])
        f2b_ref[...] = _bf(f2_ref[...])


def _outln_kernel(y_ref, w_ref, x_ref, gt_ref, g_ref, b_ref, o_ref):
    sub = y_ref.shape[0] // OUTLN_SUBTILES
    for s in range(OUTLN_SUBTILES):
        rs = slice(s * sub, (s + 1) * sub)
        r = DEEPNORM_ALPHA * x_ref[rs, :] + (1.0 + gt_ref[...]) * _dot(y_ref[rs, :], w_ref[...])
        o_ref[rs, :] = _layer_norm_rows(r, g_ref[...], b_ref[...])


def _outln_call(y, w, slot, x, modv, mlayer, lng, lnb, layer, seq):
    n, d = x.shape
    kdim = y.shape[1]
    tm = min(512, seq)
    tpb = seq // tm
    return pl.pallas_call(
        _outln_kernel,
        grid=(n // tm,),
        in_specs=[
            pl.BlockSpec((tm, kdim), lambda i: (i, 0)),
            pl.BlockSpec((None, kdim, d), lambda i: (slot, 0, 0), pipeline_mode=pl.Buffered(1)),
            pl.BlockSpec((tm, d), lambda i: (i, 0)),
            _mod_spec(mlayer, 2, tpb, d),
            _ln_spec(layer, 0, d),
            _ln_spec(layer, 0, d),
        ],
        out_specs=pl.BlockSpec((tm, d), lambda i: (i, 0)),
        out_shape=jax.ShapeDtypeStruct((n, d), F32),
        compiler_params=_params("arbitrary"),
        name="outln",
    )(y, w, x, modv, lng, lnb)


def _mlp_kernel(has_mod_job, x_ref, sh_ref, sc_ref, gt_ref, w1_ref, w2_ref, g_ref, b_ref, *rest):
    if has_mod_job:
        c_ref, wm_ref, bm_ref, o_ref, mo_ref, h_scr = rest
    else:
        o_ref, h_scr = rest
    k = pl.program_id(1)

    @pl.when(k == 0)
    def _():
        h_scr[...] = _bf(x_ref[...] * (1.0 + sc_ref[...]) + sh_ref[...])
        o_ref[...] = jnp.zeros_like(o_ref)

    hid = jnp.maximum(_dot(h_scr[...], w1_ref[...]), 0.0)
    o_ref[...] += _dot(_bf(hid * hid), w2_ref[...])
    if has_mod_job:
        mo_ref[...] = _dot(_bf(_silu(c_ref[...])), _bf(wm_ref[...])) + bm_ref[...]

    @pl.when(k == pl.num_programs(1) - 1)
    def _():
        r = DEEPNORM_ALPHA * x_ref[...] + (1.0 + gt_ref[...]) * o_ref[...]
        o_ref[...] = _layer_norm_rows(r, g_ref[...], b_ref[...])


def _mlp_call(x, modv, mlayer, w1, w2, lng, lnb, layer, seq, mod_job=None):
    n, d = x.shape
    dff = w1.shape[1]
    tm = min(512, seq)
    tf = 1024
    tpb = seq // tm
    ksteps = dff // tf
    in_specs = [
        pl.BlockSpec((tm, d), lambda i, k: (i, 0)),
        _mod_spec(mlayer, 3, tpb, d),
        _mod_spec(mlayer, 4, tpb, d),
        _mod_spec(mlayer, 5, tpb, d),
        pl.BlockSpec((d, tf), lambda i, k: (0, k)),
        pl.BlockSpec((tf, d), lambda i, k: (k, 0)),
        _ln_spec(layer, 1, d),
        _ln_spec(layer, 1, d),
    ]
    out_specs = [pl.BlockSpec((tm, d), lambda i, k: (i, 0))]
    out_shape = [jax.ShapeDtypeStruct((n, d), F32)]
    args = [x, modv, modv, modv, w1, w2, lng, lnb]
    if mod_job is not None:
        c_pad, w_mod, b_mod, first = mod_job
        nl = w_mod.shape[0] - first
        nmod = w_mod.shape[2]
        steps = (n // tm) * ksteps
        wcol = next(w for w in (384, 768, 1536, 3072, 6144, 12288) if nl * (nmod // w) <= steps)
        per = nmod // wcol
        q = lambda i, k: jnp.minimum(i * ksteps + k, nl * per - 1)
        rows = c_pad.shape[0]
        in_specs += [
            pl.BlockSpec((rows, d), lambda i, k: (0, 0)),
            pl.BlockSpec((None, d, wcol), lambda i, k: (first + q(i, k) // per, 0, q(i, k) % per)),
            pl.BlockSpec((None, 1, wcol), lambda i, k: (first + q(i, k) // per, 0, q(i, k) % per)),
        ]
        out_specs.append(pl.BlockSpec((None, rows, wcol), lambda i, k: (q(i, k) // per, 0, q(i, k) % per)))
        out_shape.append(jax.ShapeDtypeStruct((nl, rows, nmod), F32))
        args += [c_pad, w_mod, b_mod.reshape(-1, 1, nmod)]
    outs = pl.pallas_call(
        functools.partial(_mlp_kernel, mod_job is not None),
        grid=(n // tm, ksteps),
        in_specs=in_specs,
        out_specs=out_specs,
        out_shape=out_shape,
        scratch_shapes=[pltpu.VMEM((tm, d), BF16)],
        compiler_params=_params("arbitrary", "arbitrary"),
        name="mlp",
    )(*args)
    return (outs[0], outs[1]) if mod_job is not None else (outs[0], None)


def _rglru_kernel(gate_ref, rec_ref, cw_ref, cb_ref, wg_ref, bg_ref, lam_ref, f1_ref, f2_ref,
                  o_ref, f1b_ref, f2b_ref, halo, abuf, bbuf, hcar):
    tm = rec_ref.shape[0]
    _CastJob.run(f1_ref, f2_ref, f1b_ref, f2b_ref)

    @pl.when(pl.program_id(1) == 0)
    def _():
        halo[...] = jnp.zeros_like(halo)
        hcar[...] = jnp.zeros_like(hcar)

    sp = _softplus(-lam_ref[...])
    for g in range(RNN_GROUPS):
        cs = slice(RNN_GROUP * g, RNN_GROUP * (g + 1))
        xr = _causal_conv(halo[:, cs], rec_ref[:, cs], cw_ref, cs) + cb_ref[:, cs]
        pre = _dot(_bf(xr), wg_ref[g]) + bg_ref[g]
        r_gate = _sigmoid(pre[:, :RNN_GROUP])
        i_gate = _sigmoid(pre[:, RNN_GROUP:])
        neg_log_a = LRU_C * r_gate * sp[:, cs]
        a = jnp.exp(-neg_log_a)
        abuf[:, cs] = a
        bbuf[:, cs] = jnp.sqrt(jnp.tanh(neg_log_a) * (1.0 + a * a)) * (i_gate * xr)
    halo[...] = rec_ref[tm - SUBLANES:tm, :]

    row = lax.broadcasted_iota(jnp.int32, (SUBLANES, RNN_WIDTH), 0)

    def body(j, carry):
        r0 = pl.multiple_of(j * SUBLANES, SUBLANES)
        a = abuf[pl.ds(r0, SUBLANES), :]
        b = bbuf[pl.ds(r0, SUBLANES), :]
        for k in (1, 2, 4):
            a_s = jnp.where(row >= k, pltpu.roll(a, k, 0), 1.0)
            b_s = jnp.where(row >= k, pltpu.roll(b, k, 0), 0.0)
            b = a * b_s + b
            a = a * a_s
        h = a * carry + b
        bbuf[pl.ds(r0, SUBLANES), :] = h
        return h[SUBLANES - 1:SUBLANES, :]

    hcar[...] = lax.fori_loop(0, tm // SUBLANES, body, hcar[...])
    o_ref[...] = _bf(jax.nn.gelu(gate_ref[...], approximate=True) * bbuf[...])


def _rglru_core_call(u, conv_w, conv_b, wg, bg, lam, batch, seq, layer, w_ff1, w_ff2):
    n = u.shape[0]
    tm = min(256, seq)
    tpb = seq // tm
    r = RNN_WIDTH
    const2 = lambda b, t: (0, 0)
    job = _CastJob(layer, w_ff1, w_ff2, batch * tpb, lambda b, t: b * tpb + t)
    return pl.pallas_call(
        _rglru_kernel,
        grid=(batch, tpb),
        in_specs=[
            pl.BlockSpec((tm, r), lambda b, t: (b * tpb + t, 0)),
            pl.BlockSpec((tm, r), lambda b, t: (b * tpb + t, 1)),
            pl.BlockSpec((CONV_WIDTH, r), const2),
            pl.BlockSpec((1, r), const2),
            pl.BlockSpec((RNN_GROUPS, RNN_GROUP, 2 * RNN_GROUP), lambda b, t: (0, 0, 0)),
            pl.BlockSpec((RNN_GROUPS, 1, 2 * RNN_GROUP), lambda b, t: (0, 0, 0)),
            pl.BlockSpec((1, r), const2),
        ] + job.in_specs,
        out_specs=[pl.BlockSpec((tm, r), lambda b, t: (b * tpb + t, 0))] + job.out_specs,
        out_shape=[jax.ShapeDtypeStruct((n, r), BF16)] + job.out_shape,
        scratch_shapes=[
            pltpu.VMEM((SUBLANES, r), F32),
            pltpu.VMEM((tm, r), F32),
            pltpu.VMEM((tm, r), F32),
            pltpu.VMEM((1, r), F32),
        ],
        compiler_params=_params("arbitrary", "arbitrary"),
        name="rglru_core",
    )(u, u, conv_w, conv_b, wg, bg, lam, *job.args)


def _gdn_kernel(qkv_ref, z_ref, br_ref, ar_ref, cw_ref, alog_ref, dtb_ref, nw_ref, f1_ref, f2_ref,
                o_ref, f1b_ref, f2b_ref, halo, qs, ks, vs, gcb, btb, gct, s_scr, o_scr):
    c = CHUNK
    hd = GDN_HEAD_DIM
    _CastJob.run(f1_ref, f2_ref, f1b_ref, f2b_ref)

    @pl.when(pl.program_id(1) == 0)
    def _():
        halo[...] = jnp.zeros_like(halo)
        s_scr[...] = jnp.zeros_like(s_scr)

    beta = _sigmoid(br_ref[...])
    g = -jnp.exp(alog_ref[...]) * _softplus(ar_ref[...] + dtb_ref[...])
    ti64 = lax.broadcasted_iota(jnp.int32, (c, c), 0)
    tj64 = lax.broadcasted_iota(jnp.int32, (c, c), 1)
    tri = jnp.where(ti64 >= tj64, 1.0, 0.0).astype(BF16)
    gc = _dot_mask_lhs(tri, g)
    g_t = jnp.concatenate([g, jnp.zeros_like(g)], axis=0).T
    ut = lax.broadcasted_iota(jnp.int32, (2 * c, 2 * c), 0)
    uj = lax.broadcasted_iota(jnp.int32, (2 * c, 2 * c), 1) & (c - 1)
    upper2 = jnp.where((ut <= uj) & (ut < c), 1.0, 0.0).astype(BF16)
    gct[...] = _dot_mask_rhs(g_t, upper2)
    for h in range(GDN_V_HEADS):
        gcb[h] = jnp.broadcast_to(gc[:, h:h + 1], (c, LANES))
        btb[h] = jnp.broadcast_to(beta[:, h:h + 1], (c, LANES))

    def conv_silu(col0, width):
        cs = slice(col0, col0 + width)
        return _silu(_causal_conv(halo[:, cs], qkv_ref[:, cs], cw_ref, cs))

    for p in range(GDN_PAIRS):
        q = conv_silu(hd * p, hd)
        q = q * lax.rsqrt(jnp.sum(q * q, axis=-1, keepdims=True) + RMS_EPS) * (hd ** -0.5)
        k = conv_silu(GDN_KEY_DIM + hd * p, hd)
        k = k * lax.rsqrt(jnp.sum(k * k, axis=-1, keepdims=True) + RMS_EPS)
        qs[p] = q
        ks[p] = k
        vs[p] = conv_silu(2 * GDN_KEY_DIM + 2 * hd * p, 2 * hd)
    halo[...] = qkv_ref[c - SUBLANES:c, :]

    ti = lax.broadcasted_iota(jnp.int32, (c, LANES), 0)
    lane = lax.broadcasted_iota(jnp.int32, (c, LANES), 1)
    left = lane < c
    tj = lane & (c - 1)
    left_row = lax.broadcasted_iota(jnp.int32, (1, LANES), 1) < c
    eye = jnp.where(ti == tj, 1.0, 0.0)
    left2 = lax.broadcasted_iota(jnp.int32, (c, 2 * hd), 1) < hd
    left2s = lax.broadcasted_iota(jnp.int32, (hd, 2 * hd), 1) < hd
    zero_tile = jnp.zeros((c, hd), F32)

    def block_diag(x, msk):
        return jnp.concatenate([jnp.where(msk, x, 0.0), jnp.where(msk, 0.0, x)], axis=0)

    def gc_row(p):
        return jnp.where(left_row, gct[2 * p:2 * p + 1, :], gct[2 * p + 1:2 * p + 2, :])

    for g0 in range(0, GDN_PAIRS, GDN_GROUP):
        grp = range(g0, g0 + GDN_GROUP)
        kq, a_qk, w_pow, t_inv, wu, r1, r2, v_new = {}, {}, {}, {}, {}, {}, {}, {}
        for p in grp:
            k = ks[p]
            kq[p] = _dot_nt(_bf(jnp.concatenate([k, qs[p]], axis=0)),
                            _bf(jnp.concatenate([k, k], axis=0)))
        for p in grp:
            col = jnp.where(left, gcb[2 * p], gcb[2 * p + 1])
            decay = jnp.where(ti >= tj, jnp.exp(jnp.minimum(col - gc_row(p), 0.0)), 0.0)
            a_kk = (jnp.where(ti > tj, kq[p][:c] * decay, 0.0)
                    * jnp.where(left, btb[2 * p], btb[2 * p + 1]))
            a_qk[p] = kq[p][c:] * decay
            w_pow[p] = _dot(_bf(a_kk), _bf(block_diag(a_kk, left)))
            t_inv[p] = eye - a_kk
        for _ in range(4):
            for p in grp:
                r = _dot(_bf(jnp.concatenate([w_pow[p], t_inv[p]], axis=0)),
                         _bf(block_diag(w_pow[p], left)))
                w_pow[p] = r[:c]
                t_inv[p] = t_inv[p] + r[c:]
        for p in grp:
            t_inv[p] = t_inv[p] + _dot(_bf(t_inv[p]), _bf(block_diag(w_pow[p], left)))
        for p in grp:
            k = ks[p]
            bt0 = btb[2 * p]
            bt1 = btb[2 * p + 1]
            v_ss = vs[p]
            rhs = jnp.concatenate([
                jnp.concatenate([k * (bt0 * jnp.exp(gcb[2 * p])), zero_tile,
                                 v_ss[:, :hd] * bt0, zero_tile], axis=1),
                jnp.concatenate([zero_tile, k * (bt1 * jnp.exp(gcb[2 * p + 1])),
                                 zero_tile, v_ss[:, hd:] * bt1], axis=1)], axis=0)
            wu[p] = _dot(_bf(t_inv[p]), _bf(rhs))
        for p in grp:
            q = qs[p]
            q_dec = jnp.concatenate([q * jnp.exp(gcb[2 * p]), q * jnp.exp(gcb[2 * p + 1])], axis=1)
            r1[p] = _dot(_bf(jnp.concatenate([wu[p][:, :2 * hd], q_dec], axis=0)),
                         _bf(block_diag(s_scr[p], left2s)))
        for p in grp:
            v_new[p] = wu[p][:, 2 * hd:] - r1[p][:c]
            k = ks[p]
            gl = jnp.where(left_row, gcb[2 * p][c - 1:c, :], gcb[2 * p + 1][c - 1:c, :])
            k_tail = jnp.concatenate([k, k], axis=0).T * jnp.exp(gl - gc_row(p))
            r2[p] = _dot(_bf(jnp.concatenate([a_qk[p], k_tail], axis=0)),
                         _bf(block_diag(v_new[p], left2)))
        for p in grp:
            o_scr[p] = r1[p][c:] + r2[p][:c]
            cd = jnp.concatenate([jnp.exp(gcb[2 * p][c - 1:c, :]),
                                  jnp.exp(gcb[2 * p + 1][c - 1:c, :])], axis=1)
            s_scr[p] = s_scr[p] * cd + r2[p][c:]

    nw = nw_ref[...]
    for p in range(GDN_PAIRS):
        o_ss = o_scr[p]
        for s in range(2):
            cs = slice(2 * hd * p + hd * s, 2 * hd * p + hd * (s + 1))
            o = o_ss[:, hd * s:hd * (s + 1)]
            nrm = o * lax.rsqrt(jnp.mean(o * o, axis=-1, keepdims=True) + RMS_EPS)
            o_ref[:, cs] = _bf(nrm * nw * _silu(z_ref[:, cs]))


def _gdn_core_call(u, u_gate, conv_w, a_log, dt_bias, norm_w, batch, seq, layer, w_ff1, w_ff2):
    n = u.shape[0]
    c = CHUNK
    cpb = seq // c
    hd = GDN_HEAD_DIM
    row = lambda b, t: b * cpb + t
    const2 = lambda b, t: (0, 0)
    job = _CastJob(layer, w_ff1, w_ff2, batch * cpb, row)
    return pl.pallas_call(
        _gdn_kernel,
        grid=(batch, cpb),
        in_specs=[
            pl.BlockSpec((c, GDN_CONV_DIM), lambda b, t: (row(b, t), 0)),
            pl.BlockSpec((c, GDN_VALUE_DIM), lambda b, t: (row(b, t), GDN_CONV_DIM // GDN_VALUE_DIM)),
            pl.BlockSpec((c, LANES), lambda b, t: (row(b, t), 0)),
            pl.BlockSpec((c, LANES), lambda b, t: (row(b, t), 1)),
            pl.BlockSpec((CONV_WIDTH, GDN_CONV_DIM), const2),
            pl.BlockSpec((1, LANES), const2),
            pl.BlockSpec((1, LANES), const2),
            pl.BlockSpec((1, hd), const2),
        ] + job.in_specs,
        out_specs=[pl.BlockSpec((c, GDN_VALUE_DIM), lambda b, t: (row(b, t), 0))] + job.out_specs,
        out_shape=[jax.ShapeDtypeStruct((n, GDN_VALUE_DIM), BF16)] + job.out_shape,
        scratch_shapes=[
            pltpu.VMEM((SUBLANES, GDN_CONV_DIM), F32),
            pltpu.VMEM((GDN_PAIRS, c, hd), F32),
            pltpu.VMEM((GDN_PAIRS, c, hd), F32),
            pltpu.VMEM((GDN_PAIRS, c, 2 * hd), F32),
            pltpu.VMEM((GDN_V_HEADS, c, LANES), F32),
            pltpu.VMEM((GDN_V_HEADS, c, LANES), F32),
            pltpu.VMEM((2 * c, 2 * c), F32),
            pltpu.VMEM((GDN_PAIRS, hd, 2 * hd), F32),
            pltpu.VMEM((GDN_PAIRS, c, 2 * hd), F32),
        ],
        compiler_params=_params("arbitrary", "arbitrary"),
        name="gdn_core",
    )(u, u, u_gate, u_gate, conv_w, a_log, dt_bias, norm_w, *job.args)


def _gla_kernel(q_ref, k_ref, v_ref, z_ref, al_ref, wup_ref, bal_ref, nw_ref, f1_ref, f2_ref,
                o_ref, f1b_ref, f2b_ref, s_scr, bc_scr):
    c = CHUNK
    dk = GLA_HEAD_K
    dv = GLA_HEAD_V
    _CastJob.run(f1_ref, f2_ref, f1b_ref, f2b_ref)

    nb = q_ref.shape[0]

    @pl.when(pl.program_id(0) == 0)
    def _():
        s_scr[...] = jnp.zeros_like(s_scr)

    ti = lax.broadcasted_iota(jnp.int32, (c, c), 0)
    tj = lax.broadcasted_iota(jnp.int32, (c, c), 1)
    causal = ti >= tj
    tri = jnp.where(causal, 1.0, 0.0).astype(BF16)
    nw = nw_ref[...]
    chains = [(b, h) for b in range(nb) for h in range(GLA_HEADS)]
    pre = {}
    for b in range(nb):
        pre[b] = _dot(_bf(al_ref[b]), wup_ref[...]) + bal_ref[...]
    for b in range(nb):
        log_alpha = (-1.0 / GLA_GATE_NORMALIZER) * _softplus(-pre[b])
        bc_scr[b] = _dot_mask_lhs(tri, log_alpha)
    q_dec, a_qk, o = {}, {}, {}
    for b, h in chains:
        ks_ = slice(dk * h, dk * (h + 1))
        bcum = bc_scr[b, :, ks_]
        q_dec[b, h] = _bf(q_ref[b, :, ks_] * (dk ** -0.5) * jnp.exp(bcum))
        a_qk[b, h] = _dot_nt(q_dec[b, h], _bf(k_ref[b, :, ks_] * jnp.exp(-bcum)))
    for b, h in chains:
        v = _bf(v_ref[b, :, dv * h:dv * (h + 1)])
        s_t = s_scr[b * GLA_HEADS + h]
        o[b, h] = (_dot_nt(q_dec[b, h], _bf(s_t))
                   + _dot(_bf(jnp.where(causal, a_qk[b, h], 0.0)), v))
    for b, h in chains:
        ks_ = slice(dk * h, dk * (h + 1))
        bcum = bc_scr[b, :, ks_]
        b_last = bcum[c - 1:c, :]
        k_tail = _bf(k_ref[b, :, ks_] * jnp.exp(b_last - bcum))
        i = b * GLA_HEADS + h
        s_scr[i] = s_scr[i] * jnp.exp(b_last) + _dot_tn(_bf(v_ref[b, :, dv * h:dv * (h + 1)]), k_tail)
    for b, h in chains:
        vs_ = slice(dv * h, dv * (h + 1))
        oh = o[b, h]
        nrm = oh * lax.rsqrt(jnp.mean(oh * oh, axis=-1, keepdims=True) + RMS_EPS)
        o_ref[b, :, vs_] = _bf(nrm * nw * _silu(z_ref[b, :, vs_]))


def _gla_core_call(u, u_gate, w_up, b_alpha, norm_w, batch, seq, layer, w_ff1, w_ff2):
    c = CHUNK
    u3 = u.reshape(batch, seq, u.shape[1])
    g3 = u_gate.reshape(batch, seq, u_gate.shape[1])
    const2 = lambda t: (0, 0)
    job = _CastJob(layer, w_ff1, w_ff2, seq // c, lambda t: t)
    out, w1b, w2b = pl.pallas_call(
        _gla_kernel,
        grid=(seq // c,),
        in_specs=[
            pl.BlockSpec((batch, c, GLA_KEY_DIM), lambda t: (0, t, 0)),
            pl.BlockSpec((batch, c, GLA_KEY_DIM), lambda t: (0, t, 1)),
            pl.BlockSpec((batch, c, GLA_VALUE_DIM), lambda t: (0, t, 1)),
            pl.BlockSpec((batch, c, GLA_VALUE_DIM), lambda t: (0, t, 2)),
            pl.BlockSpec((batch, c, LANES), lambda t: (0, t, 0)),
            pl.BlockSpec((LANES, GLA_KEY_DIM), const2),
            pl.BlockSpec((1, GLA_KEY_DIM), const2),
            pl.BlockSpec((1, GLA_HEAD_V), const2),
        ] + job.in_specs,
        out_specs=[pl.BlockSpec((batch, c, GLA_VALUE_DIM), lambda t: (0, t, 0))] + job.out_specs,
        out_shape=[jax.ShapeDtypeStruct((batch, seq, GLA_VALUE_DIM), BF16)] + job.out_shape,
        scratch_shapes=[pltpu.VMEM((batch * GLA_HEADS, GLA_HEAD_V, GLA_HEAD_K), F32),
                        pltpu.VMEM((batch, c, GLA_KEY_DIM), F32)],
        compiler_params=_params("arbitrary"),
        name="gla_core",
    )(u3, u3, u3, u3, g3, w_up, b_alpha, norm_w, *job.args)
    return out.reshape(batch * seq, GLA_VALUE_DIM), w1b, w2b


def _pad_cols(w, total):
    return jnp.pad(w, ((0, 0), (0, total - w.shape[1])))


def _rglru_gate_weights(w_rgate, b_rgate, w_igate, b_igate):
    per = RNN_GROUP // RNN_BLOCK_DIM
    eye = jnp.eye(per, dtype=F32)

    def dense(w):
        w = w.reshape(RNN_GROUPS, per, RNN_BLOCK_DIM, RNN_BLOCK_DIM)
        return jnp.einsum("gjde,jk->gjdke", w, eye).reshape(RNN_GROUPS, RNN_GROUP, RNN_GROUP)

    wg = jnp.concatenate([dense(w_rgate), dense(w_igate)], axis=-1).astype(BF16)
    bg = jnp.concatenate([b_rgate.reshape(RNN_GROUPS, 1, RNN_GROUP),
                          b_igate.reshape(RNN_GROUPS, 1, RNN_GROUP)], axis=-1)
    return wg, bg


def _gdn_gate_weights(w_in):
    main = GDN_CONV_DIM + GDN_VALUE_DIM
    b_w = _pad_cols(w_in[:, main:main + GDN_V_HEADS], LANES)
    a_w = _pad_cols(w_in[:, main + GDN_V_HEADS:main + 2 * GDN_V_HEADS], LANES)
    return jnp.concatenate([b_w, a_w], axis=1)


def _gla_gate_weights(w_in):
    main = 2 * GLA_KEY_DIM + 2 * GLA_VALUE_DIM
    return _pad_cols(w_in[:, main:main + GLA_GATE_RANK], LANES)


def kernel(x, c, ln_g, ln_b, w_mod, b_mod, w_ff1, w_ff2, rglru_w_in, rglru_conv_w, rglru_conv_b, rglru_w_rgate, rglru_b_rgate, rglru_w_igate, rglru_b_igate, rglru_lambda, rglru_w_out, gdn_w_in, gdn_conv_w, gdn_a_log, gdn_dt_bias, gdn_norm_w, gdn_w_out, gla_w_in, gla_w_alpha_up, gla_b_alpha, gla_norm_w, gla_w_out):
    batch, seq, d = x.shape
    depth = w_mod.shape[0]
    n = batch * seq
    xf = x.reshape(n, d)

    assert batch <= MOD_ROWS
    c_pad = jnp.pad(c, ((0, MOD_ROWS - batch), (0, 0)))
    modv = _mod_call(c_pad, w_mod, b_mod, 1).reshape(MOD_ROWS * N_MOD, 1, d)
    mlayer = 0
    lng = ln_g.reshape(depth * 2, 1, d)
    lnb = ln_b.reshape(depth * 2, 1, d)
    rglru_w_in_b = rglru_w_in.astype(BF16)
    gdn_w_in_b = gdn_w_in.astype(BF16)
    gla_w_in_b = gla_w_in.astype(BF16)
    rglru_w_out_b = rglru_w_out.astype(BF16)
    gdn_w_out_b = gdn_w_out.astype(BF16)
    gla_w_out_b = gla_w_out.astype(BF16)
    gdn_main = GDN_CONV_DIM + GDN_VALUE_DIM
    gla_main = 2 * GLA_KEY_DIM + 2 * GLA_VALUE_DIM

    for i in range(depth):
        kind, slot = i % N_MIXERS, i // N_MIXERS
        if kind == 0:
            u, _ = _proj_call(xf, modv, mlayer, rglru_w_in_b, slot, 2 * RNN_WIDTH, seq)
            wg, bg = _rglru_gate_weights(rglru_w_rgate[slot], rglru_b_rgate[slot],
                                         rglru_w_igate[slot], rglru_b_igate[slot])
            y, w1b, w2b = _rglru_core_call(u, rglru_conv_w[slot], rglru_conv_b[slot][None], wg, bg,
                                           rglru_lambda[slot][None], batch, seq, i, w_ff1, w_ff2)
            w_out = rglru_w_out_b
        elif kind == 1:
            u, u_gate = _proj_call(xf, modv, mlayer, gdn_w_in_b, slot, gdn_main, seq,
                                   w_small=_gdn_gate_weights(gdn_w_in_b[slot]))
            y, w1b, w2b = _gdn_core_call(u, u_gate, gdn_conv_w[slot],
                                         _pad_cols(gdn_a_log[slot][None], LANES),
                                         _pad_cols(gdn_dt_bias[slot][None], LANES),
                                         gdn_norm_w[slot][None], batch, seq, i, w_ff1, w_ff2)
            w_out = gdn_w_out_b
        else:
            u, u_gate = _proj_call(xf, modv, mlayer, gla_w_in_b, slot, gla_main, seq,
                                   w_small=_gla_gate_weights(gla_w_in_b[slot]))
            w_up = jnp.pad(gla_w_alpha_up[slot], ((0, LANES - GLA_GATE_RANK), (0, 0))).astype(BF16)
            y, w1b, w2b = _gla_core_call(u, u_gate, w_up, gla_b_alpha[slot][None], gla_norm_w[slot][None],
                                         batch, seq, i, w_ff1, w_ff2)
            w_out = gla_w_out_b
        xf = _outln_call(y, w_out, slot, xf, modv, mlayer, lng, lnb, i, seq)
        job = (c_pad, w_mod, b_mod, 1) if (i == 0 and depth > 1) else None
        xf, mod_rest = _mlp_call(xf, modv, mlayer, w1b, w2b, lng, lnb, i, seq, mod_job=job)
        if job is not None:
            modv = mod_rest.reshape((depth - 1) * MOD_ROWS * N_MOD, 1, d)
        mlayer = i
    return xf.reshape(batch, seq, d)
```
